```python
import jax, jax.numpy as jnp
from jax import lax
import numpy as np

D_MODEL = 2048
BATCH = 32
SEQ = 256
DEPTH = 2
DEC_BATCH = 2
DEC_SEQ = 4096
PAST_LEN = 256

GRID_W = 64
N_MIXERS = 2
N_NA_LAYERS = (DEPTH + 1) // 2
N_RET_LAYERS = DEPTH // 2
NA_HEADS = 16
NA_HEAD_DIM = D_MODEL // NA_HEADS
NA_KH = 8
NA_KW = 16
RET_HEADS = 8
RET_DK = D_MODEL // RET_HEADS
RET_DV = D_MODEL // RET_HEADS
RET_CHUNK = 128
FFN_HIDDEN = -(-8 * D_MODEL // (3 * 256)) * 256
Q_BLOCK = 128
ROPE_BASE = 10000.0
EPS = 1e-6

kernel_name = 'hybrid_na_retention_diffusion_step'


def _rmsnorm(x, g):
    x32 = x.astype(jnp.float32)
    y = x32 * lax.rsqrt(jnp.mean(x32 * x32, axis=-1, keepdims=True) + EPS)
    return (y * g.astype(jnp.float32)).astype(x.dtype)


def _ada(cond, w, b):
    m = jax.nn.silu(cond) @ w + b
    return [t[:, None, :] for t in jnp.split(m, 6, axis=-1)]


def _axial_rope(n_tokens, dim):
    t = jnp.arange(n_tokens)
    rows = (t // GRID_W).astype(jnp.float32)
    cols = (t % GRID_W).astype(jnp.float32)
    half = dim // 2
    inv = jnp.power(ROPE_BASE, -jnp.arange(0, half, 2, dtype=jnp.float32) / half)
    ar = rows[:, None] * inv[None, :]
    ac = cols[:, None] * inv[None, :]
    ang = jnp.concatenate([ar, ar, ac, ac], axis=-1)
    return jnp.cos(ang)[:, None, :], jnp.sin(ang)[:, None, :]


def _apply_axial_rope(x, cos, sin):
    d = x.shape[-1]
    xr = x.reshape(x.shape[:-1] + (2, 2, d // 4))
    rot = jnp.stack([-xr[..., 1, :], xr[..., 0, :]], axis=-2).reshape(x.shape)
    return x * cos + rot * sin


def _ctx_attention(q, k, v):
    B, L, H, dh = q.shape
    nb = L // Q_BLOCK
    qb = jnp.moveaxis(q.reshape(B, nb, Q_BLOCK, H, dh), 1, 0)

    def blk(qi):
        s = jnp.einsum('bqhd,bkhd->bhqk', qi, k).astype(jnp.float32)
        p = jax.nn.softmax(s, axis=-1).astype(v.dtype)
        return jnp.einsum('bhqk,bkhd->bqhd', p, v)

    out = lax.map(blk, qb)
    return jnp.moveaxis(out, 0, 1).reshape(B, L, H, dh)


def _na_latent_attention(q, k, v, ck, cv, rpb):
    B, N, H, dh = q.shape
    rows = N // GRID_W
    kh = min(NA_KH, rows)
    qg = q.reshape(B, rows, GRID_W, H, dh)
    kg = k.reshape(B, rows, GRID_W, H, dh)
    vg = v.reshape(B, rows, GRID_W, H, dh)
    cols = jnp.arange(GRID_W)
    cstart = jnp.clip(cols - NA_KW // 2, 0, GRID_W - NA_KW)
    col_idx = cstart[:, None] + jnp.arange(NA_KW)[None, :]
    col_off = col_idx - cols[:, None] + NA_KW - 1

    def row_step(r):
        rs = jnp.clip(r - kh // 2, 0, rows - kh)
        qr = lax.dynamic_index_in_dim(qg, r, axis=1, keepdims=False)
        kband = lax.dynamic_slice_in_dim(kg, rs, kh, axis=1)
        vband = lax.dynamic_slice_in_dim(vg, rs, kh, axis=1)
        kw = kband[:, :, col_idx]
        vw = vband[:, :, col_idx]
        row_off = rs + jnp.arange(kh) - r + NA_KH - 1
        bias = rpb[:, row_off[None, :, None], col_off[:, None, :]]
        s_loc = jnp.einsum('bqhd,bkqwhd->bhqkw', qr, kw).astype(jnp.float32) + bias[None].astype(jnp.float32)
        s_ctx = jnp.einsum('bqhd,bchd->bhqc', qr, ck).astype(jnp.float32)
        s = jnp.concatenate([s_loc.reshape(B, H, GRID_W, kh * NA_KW), s_ctx], axis=-1)
        p = jax.nn.softmax(s, axis=-1).astype(v.dtype)
        p_loc = p[..., :kh * NA_KW].reshape(B, H, GRID_W, kh, NA_KW)
        p_ctx = p[..., kh * NA_KW:]
        return (jnp.einsum('bhqkw,bkqwhd->bqhd', p_loc, vw)
                + jnp.einsum('bhqc,bchd->bqhd', p_ctx, cv))

    out = lax.map(row_step, jnp.arange(rows))
    return jnp.moveaxis(out, 0, 1).reshape(B, N, H, dh)


def _na_qkv(h, w_qkv):
    B, N, _ = h.shape
    q, k, v = jnp.split(h @ w_qkv, 3, axis=-1)
    shp = (B, N, NA_HEADS, NA_HEAD_DIM)
    return q.reshape(shp) * (NA_HEAD_DIM ** -0.5), k.reshape(shp), v.reshape(shp)


def _na_context(h, w_qkv, w_o):
    B, L, _ = h.shape
    q, k, v = _na_qkv(h, w_qkv)
    o = _ctx_attention(q, k, v)
    return o.reshape(B, L, D_MODEL) @ w_o, k, v


def _na_latent(h, w_qkv, w_o, rpb, ck, cv):
    B, N, _ = h.shape
    q, k, v = _na_qkv(h, w_qkv)
    o = _na_latent_attention(q, k, v, ck.astype(h.dtype), cv.astype(h.dtype), rpb)
    return o.reshape(B, N, D_MODEL) @ w_o


def _retention_scan(q, k, v, log_g, s0):
    B, N, H, DK = q.shape
    DV = v.shape[-1]
    C = RET_CHUNK
    nc = N // C
    pos = jnp.arange(C, dtype=jnp.float32)
    diff = pos[:, None] - pos[None, :]
    causal = diff >= 0
    decay_in = jnp.where(causal[None], jnp.exp(jnp.where(causal, diff, 0.0)[None] * log_g[:, None, None]), 0.0)
    q_decay = jnp.exp((pos + 1.0)[:, None] * log_g[None, :])[None, :, :, None]
    k_decay = jnp.exp((C - 1.0 - pos)[:, None] * log_g[None, :])[None, :, :, None]
    chunk_decay = jnp.exp(C * log_g)[None, :, None, None]
    qs = jnp.moveaxis(q.reshape(B, nc, C, H, DK), 1, 0)
    ks = jnp.moveaxis(k.reshape(B, nc, C, H, DK), 1, 0)
    vs = jnp.moveaxis(v.reshape(B, nc, C, H, DV), 1, 0)

    def step(S, inp):
        qc, kc, vc = inp
        scores = jnp.einsum('bihd,bjhd->bhij', qc, kc) * decay_in[None]
        inner = jnp.einsum('bhij,bjhe->bihe', scores, vc)
        cross = jnp.einsum('bihd,bhde->bihe', qc, S) * q_decay
        S_new = S * chunk_decay + jnp.einsum('bjhd,bjhe->bhde', kc * k_decay, vc)
        return S_new, inner + cross

    S_fin, out = lax.scan(step, s0, (qs, ks, vs))
    return jnp.moveaxis(out, 0, 1).reshape(B, N, H, DV), S_fin


def _group_norm(o):
    mu = jnp.mean(o, axis=-1, keepdims=True)
    var = jnp.mean(jnp.square(o - mu), axis=-1, keepdims=True)
    return (o - mu) * lax.rsqrt(var + EPS)


def _retention(h, w_in, w_o, decay_logit, rope, s_f0, s_b0):
    B, N, _ = h.shape
    q, k, v, gf, gb = jnp.split((h @ w_in).astype(jnp.float32), 5, axis=-1)
    q = q.reshape(B, N, RET_HEADS, RET_DK)
    k = k.reshape(B, N, RET_HEADS, RET_DK)
    v = v.reshape(B, N, RET_HEADS, RET_DV)
    if rope is not None:
        cos, sin = rope
        q = _apply_axial_rope(q, cos, sin)
        k = _apply_axial_rope(k, cos, sin)
    q = q * (RET_DK ** -0.5)
    log_g = jax.nn.log_sigmoid(decay_logit.astype(jnp.float32))
    of, sf = _retention_scan(q, k, v, log_g[0], s_f0.astype(jnp.float32))
    ob, sb = _retention_scan(q[:, ::-1], k[:, ::-1], v[:, ::-1], log_g[1], s_b0.astype(jnp.float32))
    ob = ob[:, ::-1]
    y = (jax.nn.silu(gf) * _group_norm(of).reshape(B, N, D_MODEL)
         + jax.nn.silu(gb) * _group_norm(ob).reshape(B, N, D_MODEL))
    return y.astype(h.dtype) @ w_o, sf, sb


def _ffn(h, w_in, w_out):
    g, u = jnp.split(h @ w_in, 2, axis=-1)
    return (jax.nn.silu(g) * u) @ w_out


def setup_inputs(seed: int = 0) -> dict:
    key = jax.random.key(seed)
    ks = jax.random.split(key, 20)
    D = D_MODEL
    f32 = jnp.float32
    nrm = lambda k, shp, s: jax.random.normal(k, shp, f32) * s
    gamma0 = 1.0 - jnp.power(2.0, -5.0 - jnp.arange(RET_HEADS, dtype=f32))
    logit0 = jnp.log(gamma0 / (1.0 - gamma0))
    return {
        'x_prompt': nrm(ks[0], (BATCH, SEQ, D), 1.0),
        'x_sample': nrm(ks[1], (DEC_BATCH, DEC_SEQ, D), 1.0),
        'cache_na_k': nrm(ks[2], (DEC_BATCH, N_NA_LAYERS, PAST_LEN, NA_HEADS, NA_HEAD_DIM), 1.0),
        'cache_na_v': nrm(ks[3], (DEC_BATCH, N_NA_LAYERS, PAST_LEN, NA_HEADS, NA_HEAD_DIM), 1.0),
        'state_ret_fwd': nrm(ks[4], (DEC_BATCH, N_RET_LAYERS, RET_HEADS, RET_DK, RET_DV), 1.0),
        'state_ret_bwd': nrm(ks[5], (DEC_BATCH, N_RET_LAYERS, RET_HEADS, RET_DK, RET_DV), 1.0),
        'c': nrm(ks[6], (DEC_BATCH, D), 1.0),
        'c_ctx': nrm(ks[7], (D,), 1.0),
        'w_ada': nrm(ks[8], (DEPTH, D, 6 * D), 0.5 * D ** -0.5),
        'b_ada': nrm(ks[9], (DEPTH, 6 * D), 0.02),
        'g_norm': 1.0 + nrm(ks[10], (DEPTH, 4, D), 0.02),
        'na_w_qkv': nrm(ks[11], (N_NA_LAYERS, D, 3 * D), D ** -0.5),
        'na_w_o': nrm(ks[12], (N_NA_LAYERS, D, D), D ** -0.5),
        'na_rpb': nrm(ks[13], (N_NA_LAYERS, NA_HEADS, 2 * NA_KH - 1, 2 * NA_KW - 1), 0.1),
        'ret_w_in': nrm(ks[14], (N_RET_LAYERS, D, 5 * D), D ** -0.5),
        'ret_w_o': nrm(ks[15], (N_RET_LAYERS, D, D), D ** -0.5),
        'ret_decay_logit': logit0[None, None, :] + nrm(ks[16], (N_RET_LAYERS, 2, RET_HEADS), 0.1),
        'ffn_w_in': nrm(ks[17], (DEPTH, D, 2 * FFN_HIDDEN), D ** -0.5),
        'ffn_w_out': nrm(ks[18], (DEPTH, FFN_HIDDEN, D), FFN_HIDDEN ** -0.5),
    }


def reference(x_prompt, x_sample, cache_na_k, cache_na_v, state_ret_fwd, state_ret_bwd, c, c_ctx,
              w_ada, b_ada, g_norm, na_w_qkv, na_w_o, na_rpb, ret_w_in, ret_w_o, ret_decay_logit,
              ffn_w_in, ffn_w_out):
    xp = x_prompt
    xs = x_sample
    Bp = xp.shape[0]
    rope = _axial_rope(xs.shape[1], RET_DK)
    new_k, new_v, new_sf, new_sb = [], [], [], []
    for l in range(DEPTH):
        sh_mp, sc_mp, ga_mp, sh_fp, sc_fp, ga_fp = _ada(c_ctx[None, :], w_ada[l], b_ada[l])
        sh_ms, sc_ms, ga_ms, sh_fs, sc_fs, ga_fs = _ada(c, w_ada[l], b_ada[l])
        g = g_norm[l]
        hp = _rmsnorm(xp, g[0]) * (1.0 + sc_mp) + sh_mp
        hs = _rmsnorm(xs, g[0]) * (1.0 + sc_ms) + sh_ms
        j = l // N_MIXERS
        if l % N_MIXERS == 0:
            op, kp, vp = _na_context(hp, na_w_qkv[j], na_w_o[j])
            new_k.append(kp)
            new_v.append(vp)
            os_ = _na_latent(hs, na_w_qkv[j], na_w_o[j], na_rpb[j], cache_na_k[:, j], cache_na_v[:, j])
        else:
            zeros = jnp.zeros((Bp, RET_HEADS, RET_DK, RET_DV), jnp.float32)
            op, sf, sb = _retention(hp, ret_w_in[j], ret_w_o[j], ret_decay_logit[j], None, zeros, zeros)
            new_sf.append(sf.astype(xp.dtype))
            new_sb.append(sb.astype(xp.dtype))
            os_, _, _ = _retention(hs, ret_w_in[j], ret_w_o[j], ret_decay_logit[j], rope,
                                   state_ret_fwd[:, j], state_ret_bwd[:, j])
        xp = xp + ga_mp * _rmsnorm(op, g[1])
        xs = xs + ga_ms * _rmsnorm(os_, g[1])
        hp = _rmsnorm(xp, g[2]) * (1.0 + sc_fp) + sh_fp
        hs = _rmsnorm(xs, g[2]) * (1.0 + sc_fs) + sh_fs
        xp = xp + ga_fp * _rmsnorm(_ffn(hp, ffn_w_in[l], ffn_w_out[l]), g[3])
        xs = xs + ga_fs * _rmsnorm(_ffn(hs, ffn_w_in[l], ffn_w_out[l]), g[3])
    return (xp, xs, jnp.stack(new_k, axis=1), jnp.stack(new_v, axis=1),
            jnp.stack(new_sf, axis=1), jnp.stack(new_sb, axis=1))
```

```python
import functools

import jax
import jax.numpy as jnp
from jax import lax
from jax.experimental import pallas as pl
from jax.experimental.pallas import tpu as pltpu

D_MODEL = 2048
GRID_W = 64
NA_HEADS = 16
NA_HEAD_DIM = D_MODEL // NA_HEADS
NA_KH = 8
NA_KW = 16
RET_HEADS = 8
RET_DK = D_MODEL // RET_HEADS
RET_CHUNK = 128
FFN_HIDDEN = -(-8 * D_MODEL // (3 * 256)) * 256
ROPE_BASE = 10000.0
EPS = 1e-6

F32 = jnp.float32
BF16 = jnp.bfloat16
MASK_VALUE = -1e30

VMEM_LIMIT_BYTES = 56 * 1024 * 1024
ROW_CHUNK = 256
NA_QROWS = 4
NA_QBLK = NA_QROWS * GRID_W
NA_BAND_BLKS = 3


def _cparams(n_axes):
    return pltpu.CompilerParams(
        dimension_semantics=("arbitrary",) * n_axes,
        vmem_limit_bytes=VMEM_LIMIT_BYTES)


def _silu(x):
    return x / (1.0 + jnp.exp(-x))


def _rms(x, g):
    ms = jnp.mean(x * x, axis=-1, keepdims=True)
    return x * lax.rsqrt(ms + EPS) * g


def _dot(a, b):
    return jnp.dot(a, b, preferred_element_type=F32)


def _dot_nt(a, b):
    return lax.dot_general(a, b, (((1,), (1,)), ((), ())), preferred_element_type=F32)


def _ada_kernel(cond_ref, w_ref, b_ref, o_ref):
    a = _silu(cond_ref[...]).astype(BF16)
    o_ref[0] = _dot(a, w_ref[0].astype(BF16)) + b_ref[0]


def _ada_modulation(cond, w_ada, b_ada, tn=1024):
    depth, d, n = w_ada.shape
    rows = cond.shape[0]
    return pl.pallas_call(
        _ada_kernel,
        grid=(depth, n // tn),
        in_specs=[
            pl.BlockSpec((rows, d), lambda l, j: (0, 0)),
            pl.BlockSpec((1, d, tn), lambda l, j: (l, 0, j)),
            pl.BlockSpec((1, 1, tn), lambda l, j: (l, 0, j)),
        ],
        out_specs=pl.BlockSpec((1, rows, tn), lambda l, j: (l, 0, j)),
        out_shape=jax.ShapeDtypeStruct((depth, rows, n), F32),
        compiler_params=_cparams(2),
        name="ada_modulation",
    )(cond, w_ada, b_ada.reshape(depth, 1, n))


def _norm_mod_rows(x_ref, g, sc, sh, h_ref):
    tm = x_ref.shape[0]
    for r in range(tm // ROW_CHUNK):
        sl = slice(r * ROW_CHUNK, (r + 1) * ROW_CHUNK)
        h_ref[sl, :] = (_rms(x_ref[sl, :], g) * (1.0 + sc) + sh).astype(BF16)


def _qkv_kernel(x_ref, mod_ref, g_ref, w_ref, *rest, nq, nk, emit_kv):
    if emit_kv:
        o_ref, k_ref, v_ref, h_scr = rest
    else:
        o_ref, h_scr = rest
    j = pl.program_id(1)

    @pl.when(j == 0)
    def _():
        _norm_mod_rows(x_ref, g_ref[0:1, :], mod_ref[0, 1:2, :], mod_ref[0, 0:1, :], h_scr)

    acc = _dot(h_scr[...], w_ref[...])
    scale = jnp.where(j < nq, NA_HEAD_DIM ** -0.5, 1.0).astype(F32)
    o_ref[...] = (acc * scale).astype(BF16)
    if emit_kv:
        @pl.when((j >= nq) & (j < nq + nk))
        def _():
            k_ref[...] = acc

        @pl.when(j >= nq + nk)
        def _():
            v_ref[...] = acc


def _qkv_proj(x, mod, g, w, *, emit_kv, tn=1024):
    t, d = x.shape
    tm = 512 if emit_kv else 1024
    n = w.shape[1]
    groups = mod.shape[0]
    tiles_per_group = t // groups // tm
    nq = d // tn
    nk = d // tn
    out_shape = [jax.ShapeDtypeStruct((t, n), BF16)]
    out_specs = [pl.BlockSpec((tm, tn), lambda i, j: (i, j))]
    if emit_kv:
        out_shape += [jax.ShapeDtypeStruct((t, d), F32)] * 2
        out_specs += [
            pl.BlockSpec((tm, tn), lambda i, j: (i, jnp.clip(j - nq, 0, nk - 1))),
            pl.BlockSpec((tm, tn), lambda i, j: (i, jnp.clip(j - nq - nk, 0, nk - 1))),
        ]
    return pl.pallas_call(
        functools.partial(_qkv_kernel, nq=nq, nk=nk, emit_kv=emit_kv),
        grid=(t // tm, n // tn),
        in_specs=[
            pl.BlockSpec((tm, d), lambda i, j: (i, 0)),
            pl.BlockSpec((1, 6, d), lambda i, j: (i // tiles_per_group, 0, 0)),
            pl.BlockSpec((4, d), lambda i, j: (0, 0)),
            pl.BlockSpec((d, tn), lambda i, j: (0, j)),
        ],
        out_specs=out_specs,
        out_shape=out_shape,
        scratch_shapes=[pltpu.VMEM((tm, d), BF16)],
        compiler_params=_cparams(2),
        name="qkv_proj_kv" if emit_kv else "qkv_proj",
    )(x, mod, g, w)


def _ctx_attn_kernel(q_ref, k_ref, v_ref, o_ref):
    for h in range(NA_HEADS):
        sl = slice(h * NA_HEAD_DIM, (h + 1) * NA_HEAD_DIM)
        s = _dot_nt(q_ref[:, sl], k_ref[:, sl])
        m = jnp.max(s, axis=-1, keepdims=True)
        p = jnp.exp(s - m)
        l = jnp.sum(p, axis=-1, keepdims=True)
        o = _dot(p.astype(BF16), v_ref[:, sl])
        o_ref[:, sl] = (o * (1.0 / l)).astype(BF16)


def _ctx_attention(qkv, seq):
    t = qkv.shape[0]
    d = D_MODEL
    return pl.pallas_call(
        _ctx_attn_kernel,
        grid=(t // seq,),
        in_specs=[
            pl.BlockSpec((seq, d), lambda b: (b, 0)),
            pl.BlockSpec((seq, d), lambda b: (b, 1)),
            pl.BlockSpec((seq, d), lambda b: (b, 2)),
        ],
        out_specs=pl.BlockSpec((seq, d), lambda b: (b, 0)),
        out_shape=jax.ShapeDtypeStruct((t, d), BF16),
        compiler_params=_cparams(1),
        name="ctx_attention",
    )(qkv, qkv, qkv)


def _band_start_blk(rb, n_rb):
    return jnp.clip(rb - 1, 0, n_rb - NA_BAND_BLKS)


def _bias_tile_plan(n_grid_rows):
    n_rb = n_grid_rows // NA_QROWS
    plans = []
    for rb in (0, 1, n_rb - 1):
        r0 = rb * NA_QROWS
        a = min(max(rb - 1, 0), n_rb - NA_BAND_BLKS) * NA_QROWS
        rows = []
        for i in range(NA_QROWS):
            r = r0 + i
            rs = min(max(r - NA_KH // 2, 0), n_grid_rows - NA_KH)
            pairs = []
            for jp in range(NA_BAND_BLKS * NA_QROWS // 2):
                drs = []
                for jj in (2 * jp, 2 * jp + 1):
                    kr = a + jj
                    drs.append(kr - r + NA_KH - 1 if rs <= kr < rs + NA_KH else None)
                pairs.append(tuple(drs))
            rows.append(pairs)
        plans.append(rows)
    return plans


def _na_bias_kernel(rpb_ref, o_ref, *, plan):
    h = pl.program_id(0)
    n_dr = 2 * NA_KH - 1
    n_dc = 2 * NA_KW - 1
    shape = (GRID_W, 2 * GRID_W)
    qc = lax.broadcasted_iota(jnp.int32, shape, 0)
    lane = lax.broadcasted_iota(jnp.int32, shape, 1)
    kc = lane & (GRID_W - 1)
    right = lane >= GRID_W
    diff = kc - qc
    cstart = jnp.clip(qc - NA_KW // 2, 0, GRID_W - NA_KW)
    in_window = (kc >= cstart) & (kc < cstart + NA_KW)
    masked = jnp.full(shape, MASK_VALUE, F32)

    used = sorted({dr for rows in plan for pairs in rows for pr in pairs for dr in pr if dr is not None})
    tiles = {}
    for dr in used:
        t = masked
        for dc in range(n_dc):
            val = rpb_ref[h * (n_dr * n_dc) + dr * n_dc + dc]
            t = jnp.where(diff == dc - (NA_KW - 1), val, t)
        tiles[dr] = jnp.where(in_window, t, MASK_VALUE)
    tiles[None] = masked

    for ty, rows in enumerate(plan):
        for i, pairs in enumerate(rows):
            for jp, (dl, dr) in enumerate(pairs):
                tile = tiles[dl] if dl == dr else jnp.where(right, tiles[dr], tiles[dl])
                o_ref[ty, 0, i * GRID_W:(i + 1) * GRID_W, jp * 2 * GRID_W:(jp + 1) * 2 * GRID_W] = tile


def _na_bias_table(rpb, n_grid_rows):
    heads = rpb.shape[0]
    band = NA_BAND_BLKS * NA_QBLK
    plan = _bias_tile_plan(n_grid_rows)
    return pl.pallas_call(
        functools.partial(_na_bias_kernel, plan=plan),
        grid=(heads,),
        in_specs=[pl.BlockSpec(memory_space=pltpu.SMEM)],
        out_specs=pl.BlockSpec((3, 1, NA_QBLK, band), lambda h: (0, h, 0, 0)),
        out_shape=jax.ShapeDtypeStruct((3, heads, NA_QBLK, band), F32),
        compiler_params=_cparams(1),
        name="na_bias_table",
    )(rpb.reshape(-1))


def _na_attn_kernel(q_ref, k0_ref, k1_ref, k2_ref, v0_ref, v1_ref, v2_ref,
                    ck_ref, cv_ref, bias_ref, o_ref):
    q = q_ref[...]
    s = []
    for jb, k_ref in enumerate((k0_ref, k1_ref, k2_ref)):
        s.append(_dot_nt(q, k_ref[...]) + bias_ref[0, 0, :, jb * NA_QBLK:(jb + 1) * NA_QBLK])
    s.append(_dot_nt(q, ck_ref[0].astype(BF16)))
    m = s[0].max(axis=-1, keepdims=True)
    for sj in s[1:]:
        m = jnp.maximum(m, sj.max(axis=-1, keepdims=True))
    p = [jnp.exp(sj - m) for sj in s]
    l = p[0].sum(axis=-1, keepdims=True)
    for pj in p[1:]:
        l = l + pj.sum(axis=-1, keepdims=True)
    o = _dot(p[3].astype(BF16), cv_ref[0].astype(BF16))
    for pj, v_ref in zip(p[:3], (v0_ref, v1_ref, v2_ref)):
        o = o + _dot(pj.astype(BF16), v_ref[...])
    o_ref[...] = (o * (1.0 / l)).astype(BF16)


def _na_attention(qkv, ck, cv, bias, batch, n_tok):
    d = D_MODEL
    dh = NA_HEAD_DIM
    n_rb = n_tok // NA_QBLK
    past = ck.shape[1]

    def qmap(h, b, rb):
        return (b * n_rb + rb, h)

    def band_map(col0, off):
        def f(h, b, rb):
            return (b * n_rb + _band_start_blk(rb, n_rb) + off, col0 + h)
        return f

    def bias_map(h, b, rb):
        ty = jnp.where(rb == 0, 0, jnp.where(rb == n_rb - 1, 2, 1))
        return (ty, h, 0, 0)

    blk = pl.BlockSpec((NA_QBLK, dh), qmap)
    in_specs = [blk]
    in_specs += [pl.BlockSpec((NA_QBLK, dh), band_map(NA_HEADS, off)) for off in range(NA_BAND_BLKS)]
    in_specs += [pl.BlockSpec((NA_QBLK, dh), band_map(2 * NA_HEADS, off)) for off in range(NA_BAND_BLKS)]
    in_specs += [pl.BlockSpec((1, past, dh), lambda h, b, rb: (b, 0, h))] * 2
    in_specs += [pl.BlockSpec((1, 1, NA_QBLK, NA_BAND_BLKS * NA_QBLK), bias_map)]
    return pl.pallas_call(
        _na_attn_kernel,
        grid=(NA_HEADS, batch, n_rb),
        in_specs=in_specs,
        out_specs=blk,
        out_shape=jax.ShapeDtypeStruct((batch * n_tok, d), BF16),
        compiler_params=_cparams(3),
        name="na_attention",
    )(qkv, qkv, qkv, qkv, qkv, qkv, qkv, ck, cv, bias)


def _mix_out_kernel(o_ref, w_ref, x_ref, mod_ref, g_ref, x1_ref, h2_ref):
    tm = o_ref.shape[0]
    ga = mod_ref[0, 2:3, :]
    sh = mod_ref[0, 3:4, :]
    sc = mod_ref[0, 4:5, :]
    for r in range(tm // ROW_CHUNK):
        sl = slice(r * ROW_CHUNK, (r + 1) * ROW_CHUNK)
        y = _dot(o_ref[sl, :], w_ref[...])
        x1 = x_ref[sl, :] + ga * _rms(y, g_ref[1:2, :])
        x1_ref[sl, :] = x1
        h2_ref[sl, :] = (_rms(x1, g_ref[2:3, :]) * (1.0 + sc) + sh).astype(BF16)


def _mix_out(o, w, x, mod, g, tm=512):
    t, d = x.shape
    groups = mod.shape[0]
    tiles_per_group = t // groups // tm
    return pl.pallas_call(
        _mix_out_kernel,
        grid=(t // tm,),
        in_specs=[
            pl.BlockSpec((tm, d), lambda i: (i, 0)),
            pl.BlockSpec((d, d), lambda i: (0, 0)),
            pl.BlockSpec((tm, d), lambda i: (i, 0)),
            pl.BlockSpec((1, 6, d), lambda i: (i // tiles_per_group, 0, 0)),
            pl.BlockSpec((4, d), lambda i: (0, 0)),
        ],
        out_specs=[pl.BlockSpec((tm, d), lambda i: (i, 0)),
                   pl.BlockSpec((tm, d), lambda i: (i, 0))],
        out_shape=[jax.ShapeDtypeStruct((t, d), F32), jax.ShapeDtypeStruct((t, d), BF16)],
        compiler_params=_cparams(1),
        name="mix_out",
    )(o, w, x, mod, g)


def _ffn_kernel(h_ref, wg_ref, wu_ref, wo_ref, x_ref, mod_ref, g_ref, *rest, emit_next):
    if emit_next:
        modn_ref, gn_ref, x2_ref, hn_ref, acc_ref = rest
    else:
        x2_ref, acc_ref = rest
    c = pl.program_id(1)
    h = h_ref[...]
    a = (_silu(_dot(h, wg_ref[...])) * _dot(h, wu_ref[...])).astype(BF16)
    part = _dot(a, wo_ref[...])

    @pl.when(c == 0)
    def _():
        acc_ref[...] = part

    @pl.when(c != 0)
    def _():
        acc_ref[...] += part

    @pl.when(c == pl.num_programs(1) - 1)
    def _():
        tm = h_ref.shape[0]
        ga = mod_ref[0, 5:6, :]
        for r in range(tm // ROW_CHUNK):
            sl = slice(r * ROW_CHUNK, (r + 1) * ROW_CHUNK)
            x2 = x_ref[sl, :] + ga * _rms(acc_ref[sl, :], g_ref[3:4, :])
            x2_ref[sl, :] = x2
            if emit_next:
                hn = _rms(x2, gn_ref[0:1, :]) * (1.0 + modn_ref[0, 1:2, :]) + modn_ref[0, 0:1, :]
                hn_ref[sl, :] = hn.astype(BF16)


def _ffn(h, w_in, w_out, x, mod, g, mod_next=None, g_next=None, tm=512, tf=512):
    t, d = x.shape
    hidden = w_out.shape[0]
    n_c = hidden // tf
    groups = mod.shape[0]
    tiles_per_group = t // groups // tm
    emit_next = mod_next is not None
    row = pl.BlockSpec((tm, d), lambda i, c: (i, 0))
    mod_spec = pl.BlockSpec((1, 6, d), lambda i, c: (i // tiles_per_group, 0, 0))
    g_spec = pl.BlockSpec((4, d), lambda i, c: (0, 0))
    in_specs = [
        row,
        pl.BlockSpec((d, tf), lambda i, c: (0, c)),
        pl.BlockSpec((d, tf), lambda i, c: (0, n_c + c)),
        pl.BlockSpec((tf, d), lambda i, c: (c, 0)),
        row, mod_spec, g_spec,
    ]
    args = [h, w_in, w_in, w_out, x, mod, g]
    out_specs = [row]
    out_shape = [jax.ShapeDtypeStruct((t, d), F32)]
    if emit_next:
        in_specs += [mod_spec, g_spec]
        args += [mod_next, g_next]
        out_specs.append(row)
        out_shape.append(jax.ShapeDtypeStruct((t, d), BF16))
    return pl.pallas_call(
        functools.partial(_ffn_kernel, emit_next=emit_next),
        grid=(t // tm, n_c),
        in_specs=in_specs,
        out_specs=out_specs,
        out_shape=out_shape,
        scratch_shapes=[pltpu.VMEM((tm, d), F32)],
        compiler_params=_cparams(2),
        name="ffn_next" if emit_next else "ffn",
    )(*args)


def _ret_in_kernel(h_ref, w_ref, *rest, rope, n_qk, n_q, n_v):
    if rope:
        cos_ref, sin_ref, o_ref = rest
    else:
        (o_ref,) = rest
    j = pl.program_id(1)
    acc = _dot(h_ref[...], w_ref[...])
    tn = acc.shape[1]
    half = RET_DK // 2

    @pl.when(j < n_qk)
    def _():
        scale = jnp.where(j < n_q, RET_DK ** -0.5, 1.0).astype(F32)
        for hh in range(tn // RET_DK):
            x = acc[:, hh * RET_DK:(hh + 1) * RET_DK]
            if rope:
                rot = jnp.concatenate(
                    [pltpu.roll(x[:, :half], half // 2, 1), pltpu.roll(x[:, half:], half // 2, 1)], axis=1)
                x = x * cos_ref[...] + rot * sin_ref[...]
            o_ref[:, hh * RET_DK:(hh + 1) * RET_DK] = (x * scale).astype(BF16)

    @pl.when((j >= n_qk) & (j < n_qk + n_v))
    def _():
        o_ref[...] = acc.astype(BF16)

    @pl.when(j >= n_qk + n_v)
    def _():
        o_ref[...] = _silu(acc).astype(BF16)


def _ret_in_proj(h, w, rope_tables=None, seq_len=None, tm=1024, tn=1024):
    t, d = h.shape
    n = w.shape[1]
    rope = rope_tables is not None
    in_specs = [
        pl.BlockSpec((tm, d), lambda i, j: (i, 0)),
        pl.BlockSpec((d, tn), lambda i, j: (0, j)),
    ]
    args = [h, w]
    if rope:
        tiles_per_seq = seq_len // tm
        tab = pl.BlockSpec((tm, RET_DK), lambda i, j: (i % tiles_per_seq, 0))
        in_specs += [tab, tab]
        args += list(rope_tables)
    return pl.pallas_call(
        functools.partial(_ret_in_kernel, rope=rope, n_qk=2 * d // tn, n_q=d // tn, n_v=d // tn),
        grid=(t // tm, n // tn),
        in_specs=in_specs,
        out_specs=pl.BlockSpec((tm, tn), lambda i, j: (i, j)),
        out_shape=jax.ShapeDtypeStruct((t, n), BF16),
        compiler_params=_cparams(2),
        name="ret_in_rope" if rope else "ret_in",
    )(*args)


def _rope_tables(n_tokens, dim):
    tok = jnp.arange(n_tokens)
    rows = (tok // GRID_W).astype(F32)
    cols = (tok % GRID_W).astype(F32)
    half = dim // 2
    inv = jnp.power(ROPE_BASE, -jnp.arange(0, half, 2, dtype=F32) / half)
    ar = rows[:, None] * inv[None, :]
    ac = cols[:, None] * inv[None, :]
    ang = jnp.concatenate([ar, ar, ac, ac], axis=-1)
    quarter = dim // 4
    sign = jnp.where((jnp.arange(dim) // quarter) % 2 == 0, -1.0, 1.0).astype(F32)
    return jnp.cos(ang), jnp.sin(ang) * sign[None, :]


def _log_sigmoid(x):
    y = -x
    return -(jnp.maximum(y, 0.0) + jnp.log1p(jnp.exp(-jnp.abs(y))))


def _group_norm(o):
    mu = jnp.mean(o, axis=-1, keepdims=True)
    dev = o - mu
    var = jnp.mean(dev * dev, axis=-1, keepdims=True)
    return dev * lax.rsqrt(var + EPS)


def _ret_kernel(dl_ref, q_ref, k_ref, v_ref, gf_ref, gb_ref, *rest,
                n_seq, seq_len, has_state, emit_state):
    rest = list(rest)
    if has_state:
        s0f_ref, s0b_ref = rest[:2]
        rest = rest[2:]
    y_ref = rest.pop(0)
    if emit_state:
        sf_out_ref, sb_out_ref = rest[:2]
        rest = rest[2:]
    of_scr, ob_scr, sf_scr, sb_scr = rest

    head = pl.program_id(1)
    C = RET_CHUNK
    dk = RET_DK
    n_chunks = seq_len // C

    def lg(shape, direction):
        return _log_sigmoid(jnp.full(shape, dl_ref[direction, head], F32))

    ii = lax.broadcasted_iota(jnp.int32, (C, C), 0)
    jj = lax.broadcasted_iota(jnp.int32, (C, C), 1)
    causal = ii >= jj
    anti = jj >= ii
    dist = jnp.abs(ii - jj).astype(F32)
    decay_f = jnp.where(causal, jnp.exp(jnp.where(causal, dist, 0.0) * lg((C, C), 0)), 0.0)
    decay_b = jnp.where(anti, jnp.exp(jnp.where(anti, dist, 0.0) * lg((C, C), 1)), 0.0)
    pos = lax.broadcasted_iota(jnp.int32, (C, dk), 0).astype(F32)
    lgf = lg((C, dk), 0)
    lgb = lg((C, dk), 1)
    qd_f = jnp.exp((pos + 1.0) * lgf)
    kd_f = jnp.exp((C - 1.0 - pos) * lgf)
    qd_b = jnp.exp((C - pos) * lgb)
    kd_b = jnp.exp(pos * lgb)
    cd_f = jnp.exp(C * lg((1, dk), 0))
    cd_b = jnp.exp(C * lg((1, dk), 1))

    def one_chunk(row0, decay, qd, kd, cd, s_scr, o_scr):
        rows = pl.ds(row0, C)
        q = q_ref[rows, :]
        k = k_ref[rows, :]
        v = v_ref[rows, :]
        state = s_scr[...]
        scores = _dot_nt(q, k) * decay
        o = _dot(scores.astype(BF16), v) + _dot(q, state.astype(BF16)) * qd
        k_dec = (k.astype(F32) * kd).T.astype(BF16)
        s_scr[...] = state * cd + _dot(k_dec, v)
        o_scr[rows, :] = _group_norm(o)

    def one_seq(s):
        base = s * seq_len
        if has_state:
            sf_scr[...] = s0f_ref[0, 0]
            sb_scr[...] = s0b_ref[0, 0]
        else:
            sf_scr[...] = jnp.zeros((dk, dk), F32)
            sb_scr[...] = jnp.zeros((dk, dk), F32)

        def step(t):
            one_chunk(pl.multiple_of(base + t * C, C), decay_f, qd_f, kd_f, cd_f, sf_scr, of_scr)
            one_chunk(pl.multiple_of(base + (n_chunks - 1 - t) * C, C), decay_b, qd_b, kd_b, cd_b, sb_scr, ob_scr)

        if n_chunks <= 4:
            for t in range(n_chunks):
                step(t)
        else:
            lax.fori_loop(0, n_chunks, lambda t, carry: (step(t), carry)[1], 0)
        if emit_state:
            sf_out_ref[s, 0, 0] = sf_scr[...]
            sb_out_ref[s, 0, 0] = sb_scr[...]

    if n_seq == 1:
        one_seq(0)
    else:
        lax.fori_loop(0, n_seq, lambda s, carry: (one_seq(s), carry)[1], 0)

    def combine(r, carry):
        rows = pl.ds(pl.multiple_of(r * ROW_CHUNK, ROW_CHUNK), ROW_CHUNK)
        y = (gf_ref[rows, :].astype(F32) * of_scr[rows, :]
             + gb_ref[rows, :].astype(F32) * ob_scr[rows, :])
        y_ref[rows, :] = y.astype(BF16)
        return carry

    lax.fori_loop(0, n_seq * seq_len // ROW_CHUNK, combine, 0)


def _retention(proj, decay_logit, seq_len, n_seq, state_f=None, state_b=None, emit_state=False):
    t = proj.shape[0]
    d = D_MODEL
    dk = RET_DK
    heads = RET_HEADS
    rows = n_seq * seq_len
    has_state = state_f is not None

    def col(offset):
        return pl.BlockSpec((rows, dk), lambda b, h: (b, offset * heads + h))

    in_specs = [pl.BlockSpec(memory_space=pltpu.SMEM)] + [col(o) for o in range(5)]
    args = [decay_logit, proj, proj, proj, proj, proj]
    if has_state:
        st = pl.BlockSpec((1, 1, dk, dk), lambda b, h: (b, h, 0, 0))
        in_specs += [st, st]
        args += [state_f, state_b]
    out_specs = [pl.BlockSpec((rows, dk), lambda b, h: (b, h))]
    out_shape = [jax.ShapeDtypeStruct((t, d), BF16)]
    if emit_state:
        n_batch = t // seq_len
        st_out = pl.BlockSpec((n_seq, 1, 1, dk, dk), lambda b, h: (b, 0, h, 0, 0))
        out_specs += [st_out, st_out]
        out_shape += [jax.ShapeDtypeStruct((n_batch, 1, heads, dk, dk), F32)] * 2
    return pl.pallas_call(
        functools.partial(_ret_kernel, n_seq=n_seq, seq_len=seq_len,
                          has_state=has_state, emit_state=emit_state),
        grid=(t // rows, heads),
        in_specs=in_specs,
        out_specs=out_specs,
        out_shape=out_shape,
        scratch_shapes=[pltpu.VMEM((rows, dk), F32), pltpu.VMEM((rows, dk), F32),
                        pltpu.VMEM((dk, dk), F32), pltpu.VMEM((dk, dk), F32)],
        compiler_params=_cparams(2),
        name="retention_state" if emit_state else "retention",
    )(*args)


def kernel(x_prompt, x_sample, cache_na_k, cache_na_v, state_ret_fwd, state_ret_bwd, c, c_ctx,
           w_ada, b_ada, g_norm, na_w_qkv, na_w_o, na_rpb, ret_w_in, ret_w_o, ret_decay_logit,
           ffn_w_in, ffn_w_out):
    bp, seq, d = x_prompt.shape
    bs, n_tok, _ = x_sample.shape
    past = cache_na_k.shape[2]
    xp = x_prompt.reshape(bp * seq, d)
    xs = x_sample.reshape(bs * n_tok, d)

    cond = jnp.concatenate([c_ctx[None, :], c, jnp.zeros((8 - 1 - bs, d), F32)], axis=0)
    mod = _ada_modulation(cond, w_ada, b_ada).reshape(w_ada.shape[0], 8, 6, d)
    mod_p = [mod[l, 0:1] for l in range(2)]
    mod_s = [mod[l, 1:1 + bs] for l in range(2)]

    w_qkv = na_w_qkv[0].astype(BF16)
    w_na_o = na_w_o[0].astype(BF16)
    w_ret_in = ret_w_in[0].astype(BF16)
    w_ret_o = ret_w_o[0].astype(BF16)
    w_ffn_in = ffn_w_in.astype(BF16)
    w_ffn_out = ffn_w_out.astype(BF16)

    qkv_p, k_p, v_p = _qkv_proj(xp, mod_p[0], g_norm[0], w_qkv, emit_kv=True)
    (qkv_s,) = _qkv_proj(xs, mod_s[0], g_norm[0], w_qkv, emit_kv=False)
    o_p = _ctx_attention(qkv_p, seq)
    bias = _na_bias_table(na_rpb[0], n_tok // GRID_W)
    ck = cache_na_k[:, 0].reshape(bs, past, d)
    cv = cache_na_v[:, 0].reshape(bs, past, d)
    o_s = _na_attention(qkv_s, ck, cv, bias, bs, n_tok)

    xp, hp = _mix_out(o_p, w_na_o, xp, mod_p[0], g_norm[0])
    xs, hs = _mix_out(o_s, w_na_o, xs, mod_s[0], g_norm[0])
    xp, hp = _ffn(hp, w_ffn_in[0], w_ffn_out[0], xp, mod_p[0], g_norm[0], mod_p[1], g_norm[1])
    xs, hs = _ffn(hs, w_ffn_in[0], w_ffn_out[0], xs, mod_s[0], g_norm[0], mod_s[1], g_norm[1])

    proj_p = _ret_in_proj(hp, w_ret_in)
    proj_s = _ret_in_proj(hs, w_ret_in, _rope_tables(n_tok, RET_DK), n_tok)
    y_p, sf, sb = _retention(proj_p, ret_decay_logit[0], seq, 8, emit_state=True)
    (y_s,) = _retention(proj_s, ret_decay_logit[0], n_tok, 1,
                        state_f=state_ret_fwd[:, 0], state_b=state_ret_bwd[:, 0])

    xp, hp = _mix_out(y_p, w_ret_o, xp, mod_p[1], g_norm[1])
    xs, hs = _mix_out(y_s, w_ret_o, xs, mod_s[1], g_norm[1])
    (xp,) = _ffn(hp, w_ffn_in[1], w_ffn_out[1], xp, mod_p[1], g_norm[1])
    (xs,) = _ffn(hs, w_ffn_in[1], w_ffn_out[1], xs, mod_s[1], g_norm[1])

    kv_shape = (bp, 1, seq, NA_HEADS, NA_HEAD_DIM)
    return (xp.reshape(bp, seq, d), xs.reshape(bs, n_tok, d),
            k_p.reshape(kv_shape), v_p.reshape(kv_shape),
            sf.astype(x_prompt.dtype), sb.astype(x_prompt.dtype))
```

```python
import functools

import jax
import jax.numpy as jnp
from jax import lax
from jax.experimental import pallas as pl
from jax.experimental.pallas import tpu as pltpu

D_MODEL = 2048
GRID_W = 64
NA_HEADS = 16
NA_HEAD_DIM = D_MODEL // NA_HEADS
NA_KH = 8
NA_KW = 16
RET_HEADS = 8
RET_DK = D_MODEL // RET_HEADS
RET_CHUNK = 128
FFN_HIDDEN = -(-8 * D_MODEL // (3 * 256)) * 256
ROPE_BASE = 10000.0
EPS = 1e-6

F32 = jnp.float32
BF16 = jnp.bfloat16
MASK_VALUE = -1e30

VMEM_LIMIT_BYTES = 56 * 1024 * 1024
ROW_CHUNK = 256
NA_QROWS = 4
NA_QBLK = NA_QROWS * GRID_W
NA_BAND_BLKS = 3


def _cparams(n_axes):
    return pltpu.CompilerParams(
        dimension_semantics=("arbitrary",) * n_axes,
        vmem_limit_bytes=VMEM_LIMIT_BYTES)


def _silu(x):
    return x / (1.0 + jnp.exp(-x))


def _rms(x, g):
    ms = jnp.mean(x * x, axis=-1, keepdims=True)
    return x * lax.rsqrt(ms + EPS) * g


def _dot(a, b):
    return jnp.dot(a, b, preferred_element_type=F32)


def _dot_nt(a, b):
    return lax.dot_general(a, b, (((1,), (1,)), ((), ())), preferred_element_type=F32)


def _dot_tn(a, b):
    return lax.dot_general(a, b, (((0,), (0,)), ((), ())), preferred_element_type=F32)


def _ada_kernel(cond_ref, w_ref, b_ref, o_ref):
    a = _silu(cond_ref[...]).astype(BF16)
    o_ref[0] = _dot(a, w_ref[0].astype(BF16)) + b_ref[0]


def _ada_modulation(cond, w_ada, b_ada, tn=1024):
    depth, d, n = w_ada.shape
    rows = cond.shape[0]
    return pl.pallas_call(
        _ada_kernel,
        grid=(depth, n // tn),
        in_specs=[
            pl.BlockSpec((rows, d), lambda l, j: (0, 0)),
            pl.BlockSpec((1, d, tn), lambda l, j: (l, 0, j)),
            pl.BlockSpec((1, 1, tn), lambda l, j: (l, 0, j)),
        ],
        out_specs=pl.BlockSpec((1, rows, tn), lambda l, j: (l, 0, j)),
        out_shape=jax.ShapeDtypeStruct((depth, rows, n), F32),
        compiler_params=_cparams(2),
        name="ada_modulation",
    )(cond, w_ada, b_ada.reshape(depth, 1, n))


def _row_chunks(n_rows):
    return [slice(r * ROW_CHUNK, (r + 1) * ROW_CHUNK) for r in range(n_rows // ROW_CHUNK)]


def _qkv_kernel(x_ref, mod_ref, g_ref, w_ref, *rest, nq, nk, emit_kv):
    if emit_kv:
        o_ref, k_ref, v_ref, h_scr = rest
    else:
        o_ref, h_scr = rest
    j = pl.program_id(1)
    chunks = _row_chunks(x_ref.shape[0])
    q_scale = NA_HEAD_DIM ** -0.5

    @pl.when(j == 0)
    def _():
        g, sc, sh = g_ref[0:1, :], mod_ref[0, 1:2, :], mod_ref[0, 0:1, :]
        for sl in chunks:
            h = (_rms(x_ref[sl, :], g) * (1.0 + sc) + sh).astype(BF16)
            h_scr[sl, :] = h
            o_ref[sl, :] = (_dot(h, w_ref[...]) * q_scale).astype(BF16)

    @pl.when((j > 0) & (j < nq))
    def _():
        for sl in chunks:
            o_ref[sl, :] = (_dot(h_scr[sl, :], w_ref[...]) * q_scale).astype(BF16)

    @pl.when((j >= nq) & (j < nq + nk))
    def _():
        for sl in chunks:
            acc = _dot(h_scr[sl, :], w_ref[...])
            o_ref[sl, :] = acc.astype(BF16)
            if emit_kv:
                k_ref[sl, :] = acc

    @pl.when(j >= nq + nk)
    def _():
        for sl in chunks:
            acc = _dot(h_scr[sl, :], w_ref[...])
            o_ref[sl, :] = acc.astype(BF16)
            if emit_kv:
                v_ref[sl, :] = acc


def _qkv_proj(x, mod, g, w, *, emit_kv, tn=1024):
    t, d = x.shape
    tm = 512 if emit_kv else 1024
    n = w.shape[1]
    groups = mod.shape[0]
    tiles_per_group = t // groups // tm
    nq = d // tn
    nk = d // tn
    out_shape = [jax.ShapeDtypeStruct((t, n), BF16)]
    out_specs = [pl.BlockSpec((tm, tn), lambda i, j: (i, j))]
    if emit_kv:
        out_shape += [jax.ShapeDtypeStruct((t, d), F32)] * 2
        out_specs += [
            pl.BlockSpec((tm, tn), lambda i, j: (i, jnp.clip(j - nq, 0, nk - 1))),
            pl.BlockSpec((tm, tn), lambda i, j: (i, jnp.clip(j - nq - nk, 0, nk - 1))),
        ]
    return pl.pallas_call(
        functools.partial(_qkv_kernel, nq=nq, nk=nk, emit_kv=emit_kv),
        grid=(t // tm, n // tn),
        in_specs=[
            pl.BlockSpec((tm, d), lambda i, j: (i, 0)),
            pl.BlockSpec((1, 6, d), lambda i, j: (i // tiles_per_group, 0, 0)),
            pl.BlockSpec((4, d), lambda i, j: (0, 0)),
            pl.BlockSpec((d, tn), lambda i, j: (0, j)),
        ],
        out_specs=out_specs,
        out_shape=out_shape,
        scratch_shapes=[pltpu.VMEM((tm, d), BF16)],
        compiler_params=_cparams(2),
        name="qkv_proj_kv" if emit_kv else "qkv_proj",
    )(x, mod, g, w)


def _ctx_attn_kernel(q_ref, k_ref, v_ref, o_ref):
    for h in range(NA_HEADS):
        sl = slice(h * NA_HEAD_DIM, (h + 1) * NA_HEAD_DIM)
        s = _dot_nt(q_ref[:, sl], k_ref[:, sl])
        m = jnp.max(s, axis=-1, keepdims=True)
        p = jnp.exp(s - m)
        l = jnp.sum(p, axis=-1, keepdims=True)
        o = _dot(p.astype(BF16), v_ref[:, sl])
        o_ref[:, sl] = (o * (1.0 / l)).astype(BF16)


def _ctx_attention(qkv, seq):
    t = qkv.shape[0]
    d = D_MODEL
    return pl.pallas_call(
        _ctx_attn_kernel,
        grid=(t // seq,),
        in_specs=[
            pl.BlockSpec((seq, d), lambda b: (b, 0)),
            pl.BlockSpec((seq, d), lambda b: (b, 1)),
            pl.BlockSpec((seq, d), lambda b: (b, 2)),
        ],
        out_specs=pl.BlockSpec((seq, d), lambda b: (b, 0)),
        out_shape=jax.ShapeDtypeStruct((t, d), BF16),
        compiler_params=_cparams(1),
        name="ctx_attention",
    )(qkv, qkv, qkv)


def _band_start_blk(rb, n_rb):
    return jnp.clip(rb - 1, 0, n_rb - NA_BAND_BLKS)


def _bias_tile_plan(n_grid_rows):
    n_rb = n_grid_rows // NA_QROWS
    plans = []
    for rb in (0, 1, n_rb - 1):
        r0 = rb * NA_QROWS
        a = min(max(rb - 1, 0), n_rb - NA_BAND_BLKS) * NA_QROWS
        rows = []
        for j in range(NA_BAND_BLKS * NA_QROWS):
            kr = a + j
            pairs = []
            for ip in range(NA_QROWS // 2):
                drs = []
                for i in (2 * ip, 2 * ip + 1):
                    r = r0 + i
                    rs = min(max(r - NA_KH // 2, 0), n_grid_rows - NA_KH)
                    drs.append(kr - r + NA_KH - 1 if rs <= kr < rs + NA_KH else None)
                pairs.append(tuple(drs))
            rows.append(pairs)
        plans.append(rows)
    return plans


def _na_bias_kernel(rpb_ref, o_ref, *, plan):
    h = pl.program_id(0)
    n_dr = 2 * NA_KH - 1
    n_dc = 2 * NA_KW - 1
    shape = (GRID_W, 2 * GRID_W)
    kc = lax.broadcasted_iota(jnp.int32, shape, 0)
    lane = lax.broadcasted_iota(jnp.int32, shape, 1)
    qc = lane & (GRID_W - 1)
    right = lane >= GRID_W
    diff = kc - qc
    cstart = jnp.clip(qc - NA_KW // 2, 0, GRID_W - NA_KW)
    in_window = (kc >= cstart) & (kc < cstart + NA_KW)
    masked = jnp.full(shape, MASK_VALUE, F32)

    used = sorted({dr for rows in plan for pairs in rows for pr in pairs for dr in pr if dr is not None})
    tiles = {}
    for dr in used:
        t = masked
        for dc in range(n_dc):
            val = rpb_ref[h * (n_dr * n_dc) + dr * n_dc + dc]
            t = jnp.where(diff == dc - (NA_KW - 1), val, t)
        tiles[dr] = jnp.where(in_window, t, MASK_VALUE)
    tiles[None] = masked

    for ty, rows in enumerate(plan):
        for j, pairs in enumerate(rows):
            for ip, (dl, dr) in enumerate(pairs):
                tile = tiles[dl] if dl == dr else jnp.where(right, tiles[dr], tiles[dl])
                o_ref[ty, 0, j * GRID_W:(j + 1) * GRID_W, ip * 2 * GRID_W:(ip + 1) * 2 * GRID_W] = tile


def _na_bias_table(rpb, n_grid_rows):
    heads = rpb.shape[0]
    band = NA_BAND_BLKS * NA_QBLK
    plan = _bias_tile_plan(n_grid_rows)
    return pl.pallas_call(
        functools.partial(_na_bias_kernel, plan=plan),
        grid=(heads,),
        in_specs=[pl.BlockSpec(memory_space=pltpu.SMEM)],
        out_specs=pl.BlockSpec((3, 1, band, NA_QBLK), lambda h: (0, h, 0, 0)),
        out_shape=jax.ShapeDtypeStruct((3, heads, band, NA_QBLK), F32),
        compiler_params=_cparams(1),
        name="na_bias_table",
    )(rpb.reshape(-1))


def _na_attn_kernel(q_ref, k0_ref, k1_ref, k2_ref, v0_ref, v1_ref, v2_ref,
                    ck_ref, cv_ref, bias_ref, o_ref):
    dh = NA_HEAD_DIM
    for hh in range(q_ref.shape[1] // dh):
        sl = slice(hh * dh, (hh + 1) * dh)
        q = q_ref[:, sl]
        keys = [k0_ref[:, sl], k1_ref[:, sl], k2_ref[:, sl], ck_ref[0, :, sl].astype(BF16)]
        vals = [v0_ref[:, sl], v1_ref[:, sl], v2_ref[:, sl], cv_ref[0, :, sl].astype(BF16)]
        s = [_dot_nt(kj, q) for kj in keys]
        for jb in range(NA_BAND_BLKS):
            s[jb] = s[jb] + bias_ref[0, hh, jb * NA_QBLK:(jb + 1) * NA_QBLK, :]
        m = s[0].max(axis=0, keepdims=True)
        for sj in s[1:]:
            m = jnp.maximum(m, sj.max(axis=0, keepdims=True))
        p = [jnp.exp(sj - m) for sj in s]
        l = p[0].sum(axis=0, keepdims=True)
        for pj in p[1:]:
            l = l + pj.sum(axis=0, keepdims=True)
        o_t = _dot_tn(vals[0], p[0].astype(BF16))
        for vj, pj in zip(vals[1:], p[1:]):
            o_t = o_t + _dot_tn(vj, pj.astype(BF16))
        o_ref[:, sl] = (o_t * (1.0 / l)).T.astype(BF16)


def _na_attention(qkv, ck, cv, bias, batch, n_tok, heads_per_step=4):
    d = D_MODEL
    width = heads_per_step * NA_HEAD_DIM
    n_hg = NA_HEADS // heads_per_step
    n_rb = n_tok // NA_QBLK
    past = ck.shape[1]

    def qmap(hg, b, rb):
        return (b * n_rb + rb, hg)

    def band_map(col0, off):
        def f(hg, b, rb):
            return (b * n_rb + _band_start_blk(rb, n_rb) + off, col0 + hg)
        return f

    def bias_map(hg, b, rb):
        ty = jnp.where(rb == 0, 0, jnp.where(rb == n_rb - 1, 2, 1))
        return (ty, hg, 0, 0)

    blk = pl.BlockSpec((NA_QBLK, width), qmap)
    in_specs = [blk]
    in_specs += [pl.BlockSpec((NA_QBLK, width), band_map(n_hg, off)) for off in range(NA_BAND_BLKS)]
    in_specs += [pl.BlockSpec((NA_QBLK, width), band_map(2 * n_hg, off)) for off in range(NA_BAND_BLKS)]
    in_specs += [pl.BlockSpec((1, past, width), lambda hg, b, rb: (b, 0, hg))] * 2
    in_specs += [pl.BlockSpec((1, heads_per_step, NA_BAND_BLKS * NA_QBLK, NA_QBLK), bias_map)]
    return pl.pallas_call(
        _na_attn_kernel,
        grid=(n_hg, batch, n_rb),
        in_specs=in_specs,
        out_specs=blk,
        out_shape=jax.ShapeDtypeStruct((batch * n_tok, d), BF16),
        compiler_params=_cparams(3),
        name="na_attention",
    )(qkv, qkv, qkv, qkv, qkv, qkv, qkv, ck, cv, bias)


def _mix_out_kernel(o_ref, w_ref, x_ref, mod_ref, g_ref, x1_ref, h2_ref):
    tm = o_ref.shape[0]
    ga = mod_ref[0, 2:3, :]
    sh = mod_ref[0, 3:4, :]
    sc = mod_ref[0, 4:5, :]
    for r in range(tm // ROW_CHUNK):
        sl = slice(r * ROW_CHUNK, (r + 1) * ROW_CHUNK)
        y = _dot(o_ref[sl, :], w_ref[...])
        x1 = x_ref[sl, :] + ga * _rms(y, g_ref[1:2, :])
        x1_ref[sl, :] = x1
        h2_ref[sl, :] = (_rms(x1, g_ref[2:3, :]) * (1.0 + sc) + sh).astype(BF16)


def _mix_out(o, w, x, mod, g, tm=512):
    t, d = x.shape
    groups = mod.shape[0]
    tiles_per_group = t // groups // tm
    return pl.pallas_call(
        _mix_out_kernel,
        grid=(t // tm,),
        in_specs=[
            pl.BlockSpec((tm, d), lambda i: (i, 0)),
            pl.BlockSpec((d, d), lambda i: (0, 0)),
            pl.BlockSpec((tm, d), lambda i: (i, 0)),
            pl.BlockSpec((1, 6, d), lambda i: (i // tiles_per_group, 0, 0)),
            pl.BlockSpec((4, d), lambda i: (0, 0)),
        ],
        out_specs=[pl.BlockSpec((tm, d), lambda i: (i, 0)),
                   pl.BlockSpec((tm, d), lambda i: (i, 0))],
        out_shape=[jax.ShapeDtypeStruct((t, d), F32), jax.ShapeDtypeStruct((t, d), BF16)],
        compiler_params=_cparams(1),
        name="mix_out",
    )(o, w, x, mod, g)


def _ffn_kernel(h_ref, wg_ref, wu_ref, wo_ref, x_ref, mod_ref, g_ref, *rest, emit_next):
    if emit_next:
        modn_ref, gn_ref, x2_ref, hn_ref, acc_ref = rest
    else:
        x2_ref, acc_ref = rest
    c = pl.program_id(1)
    last = pl.num_programs(1) - 1
    row_chunks = [slice(r * ROW_CHUNK, (r + 1) * ROW_CHUNK) for r in range(h_ref.shape[0] // ROW_CHUNK)]

    def partial_out(sl):
        h = h_ref[sl, :]
        a = (_silu(_dot(h, wg_ref[...])) * _dot(h, wu_ref[...])).astype(BF16)
        return _dot(a, wo_ref[...])

    @pl.when(c == 0)
    def _():
        for sl in row_chunks:
            acc_ref[sl, :] = partial_out(sl)

    @pl.when((c > 0) & (c < last))
    def _():
        for sl in row_chunks:
            acc_ref[sl, :] += partial_out(sl)

    @pl.when(c == last)
    def _():
        ga = mod_ref[0, 5:6, :]
        for sl in row_chunks:
            y = acc_ref[sl, :] + partial_out(sl)
            x2 = x_ref[sl, :] + ga * _rms(y, g_ref[3:4, :])
            x2_ref[sl, :] = x2
            if emit_next:
                hn = _rms(x2, gn_ref[0:1, :]) * (1.0 + modn_ref[0, 1:2, :]) + modn_ref[0, 0:1, :]
                hn_ref[sl, :] = hn.astype(BF16)


def _ffn(h, w_in, w_out, x, mod, g, mod_next=None, g_next=None, tm=512, tf=512):
    t, d = x.shape
    hidden = w_out.shape[0]
    n_c = hidden // tf
    groups = mod.shape[0]
    tiles_per_group = t // groups // tm
    emit_next = mod_next is not None
    row = pl.BlockSpec((tm, d), lambda i, c: (i, 0))
    mod_spec = pl.BlockSpec((1, 6, d), lambda i, c: (i // tiles_per_group, 0, 0))
    g_spec = pl.BlockSpec((4, d), lambda i, c: (0, 0))
    in_specs = [
        row,
        pl.BlockSpec((d, tf), lambda i, c: (0, c)),
        pl.BlockSpec((d, tf), lambda i, c: (0, n_c + c)),
        pl.BlockSpec((tf, d), lambda i, c: (c, 0)),
        row, mod_spec, g_spec,
    ]
    args = [h, w_in, w_in, w_out, x, mod, g]
    out_specs = [row]
    out_shape = [jax.ShapeDtypeStruct((t, d), F32)]
    if emit_next:
        in_specs += [mod_spec, g_spec]
        args += [mod_next, g_next]
        out_specs.append(row)
        out_shape.append(jax.ShapeDtypeStruct((t, d), BF16))
    return pl.pallas_call(
        functools.partial(_ffn_kernel, emit_next=emit_next),
        grid=(t // tm, n_c),
        in_specs=in_specs,
        out_specs=out_specs,
        out_shape=out_shape,
        scratch_shapes=[pltpu.VMEM((tm, d), F32)],
        compiler_params=_cparams(2),
        name="ffn_next" if emit_next else "ffn",
    )(*args)


def _ret_in_kernel(h_ref, w_ref, *rest, rope, n_qk, n_q, n_v):
    if rope:
        cos_ref, sin_ref, o_ref = rest
    else:
        (o_ref,) = rest
    j = pl.program_id(1)
    tn = w_ref.shape[1]
    half = RET_DK // 2
    chunks = _row_chunks(h_ref.shape[0])

    @pl.when(j < n_qk)
    def _():
        scale = jnp.where(j < n_q, RET_DK ** -0.5, 1.0).astype(F32)
        for sl in chunks:
            acc = _dot(h_ref[sl, :], w_ref[...])
            for hh in range(tn // RET_DK):
                x = acc[:, hh * RET_DK:(hh + 1) * RET_DK]
                if rope:
                    rot = jnp.concatenate(
                        [pltpu.roll(x[:, :half], half // 2, 1), pltpu.roll(x[:, half:], half // 2, 1)], axis=1)
                    x = x * cos_ref[sl, :] + rot * sin_ref[sl, :]
                o_ref[sl, hh * RET_DK:(hh + 1) * RET_DK] = (x * scale).astype(BF16)

    @pl.when((j >= n_qk) & (j < n_qk + n_v))
    def _():
        for sl in chunks:
            o_ref[sl, :] = _dot(h_ref[sl, :], w_ref[...]).astype(BF16)

    @pl.when(j >= n_qk + n_v)
    def _():
        for sl in chunks:
            o_ref[sl, :] = _silu(_dot(h_ref[sl, :], w_ref[...])).astype(BF16)


def _ret_in_proj(h, w, rope_tables=None, seq_len=None, tm=1024, tn=1024):
    t, d = h.shape
    n = w.shape[1]
    rope = rope_tables is not None
    in_specs = [
        pl.BlockSpec((tm, d), lambda i, j: (i, 0)),
        pl.BlockSpec((d, tn), lambda i, j: (0, j)),
    ]
    args = [h, w]
    if rope:
        tiles_per_seq = seq_len // tm
        tab = pl.BlockSpec((tm, RET_DK), lambda i, j: (i % tiles_per_seq, 0))
        in_specs += [tab, tab]
        args += list(rope_tables)
    return pl.pallas_call(
        functools.partial(_ret_in_kernel, rope=rope, n_qk=2 * d // tn, n_q=d // tn, n_v=d // tn),
        grid=(t // tm, n // tn),
        in_specs=in_specs,
        out_specs=pl.BlockSpec((tm, tn), lambda i, j: (i, j)),
        out_shape=jax.ShapeDtypeStruct((t, n), BF16),
        compiler_params=_cparams(2),
        name="ret_in_rope" if rope else "ret_in",
    )(*args)


def _rope_tables(n_tokens, dim):
    tok = jnp.arange(n_tokens)
    rows = (tok // GRID_W).astype(F32)
    cols = (tok % GRID_W).astype(F32)
    half = dim // 2
    inv = jnp.power(ROPE_BASE, -jnp.arange(0, half, 2, dtype=F32) / half)
    ar = rows[:, None] * inv[None, :]
    ac = cols[:, None] * inv[None, :]
    ang = jnp.concatenate([ar, ar, ac, ac], axis=-1)
    quarter = dim // 4
    sign = jnp.where((jnp.arange(dim) // quarter) % 2 == 0, -1.0, 1.0).astype(F32)
    return jnp.cos(ang), jnp.sin(ang) * sign[None, :]


def _log_sigmoid(x):
    y = -x
    return -(jnp.maximum(y, 0.0) + jnp.log1p(jnp.exp(-jnp.abs(y))))


def _group_norm(o):
    mu = jnp.mean(o, axis=-1, keepdims=True)
    dev = o - mu
    var = jnp.mean(dev * dev, axis=-1, keepdims=True)
    return dev * lax.rsqrt(var + EPS)


def _ret_kernel(dl_ref, q_ref, k_ref, v_ref, gf_ref, gb_ref, *rest,
                n_seq, seq_len, has_state, emit_state, unroll):
    rest = list(rest)
    if has_state:
        s0f_ref, s0b_ref = rest[:2]
        rest = rest[2:]
    y_ref = rest.pop(0)
    if emit_state:
        sf_out_ref, sb_out_ref = rest[:2]
        rest = rest[2:]
    of_scr, pb_scr, st_scr = rest

    head = pl.program_id(1)
    C = RET_CHUNK
    dk = RET_DK
    n_chunks = seq_len // C

    def lg(shape, direction):
        return _log_sigmoid(jnp.full(shape, dl_ref[direction, head], F32))

    ii = lax.broadcasted_iota(jnp.int32, (C, C), 0)
    jj = lax.broadcasted_iota(jnp.int32, (C, C), 1)
    causal = ii >= jj
    anti = jj >= ii
    dist = jnp.abs(ii - jj).astype(F32)
    decay_f = jnp.where(causal, jnp.exp(jnp.where(causal, dist, 0.0) * lg((C, C), 0)), 0.0)
    decay_b = jnp.where(anti, jnp.exp(jnp.where(anti, dist, 0.0) * lg((C, C), 1)), 0.0)
    pos = lax.broadcasted_iota(jnp.int32, (C, dk), 0).astype(F32)
    lgf = lg((C, dk), 0)
    lgb = lg((C, dk), 1)
    qd_f = jnp.exp((pos + 1.0) * lgf)
    kd_f = jnp.exp((C - 1.0 - pos) * lgf)
    qd_b = jnp.exp((C - pos) * lgb)
    kd_b = jnp.exp(pos * lgb)
    cd_f = jnp.exp(C * lg((1, dk), 0))
    cd_b = jnp.exp(C * lg((1, dk), 1))

    def chunk_rows(s, t):
        return pl.ds(pl.multiple_of(s * seq_len + t * C, C), C)

    def advance(s, q, k, v, scores, qd, kd, cd):
        state = st_scr[s]
        o = _dot(scores, v) + _dot(q, state.astype(BF16)) * qd
        st_scr[s] = state * cd + _dot_tn((k.astype(F32) * kd).astype(BF16), v)
        return o

    def fwd_chunk(s, t):
        rows = chunk_rows(s, t)
        q, k, v = q_ref[rows, :], k_ref[rows, :], v_ref[rows, :]
        raw = _dot_nt(q, k)
        pb_scr[rows, :] = (raw * decay_b).astype(BF16)
        o = advance(s, q, k, v, (raw * decay_f).astype(BF16), qd_f, kd_f, cd_f)
        of_scr[rows, :] = _group_norm(o)

    def bwd_chunk(s, t):
        rows = chunk_rows(s, t)
        q, k, v = q_ref[rows, :], k_ref[rows, :], v_ref[rows, :]
        o = advance(s, q, k, v, pb_scr[rows, :], qd_b, kd_b, cd_b)
        y = (gf_ref[rows, :].astype(F32) * of_scr[rows, :]
             + gb_ref[rows, :].astype(F32) * _group_norm(o))
        y_ref[rows, :] = y.astype(BF16)

    def sweep(chunk_fn, reverse):
        def at(t):
            for s in range(n_seq):
                chunk_fn(s, n_chunks - 1 - t if reverse else t)

        if n_chunks <= 4:
            for t in range(n_chunks):
                at(t)
        else:
            lax.fori_loop(0, n_chunks, lambda t, carry: (at(t), carry)[1], 0, unroll=unroll)

    for s in range(n_seq):
        st_scr[s] = s0f_ref[s, 0] if has_state else jnp.zeros((dk, dk), F32)
    sweep(fwd_chunk, reverse=False)
    for s in range(n_seq):
        if emit_state:
            sf_out_ref[s, 0, 0] = st_scr[s]
        st_scr[s] = s0b_ref[s, 0] if has_state else jnp.zeros((dk, dk), F32)
    sweep(bwd_chunk, reverse=True)
    if emit_state:
        for s in range(n_seq):
            sb_out_ref[s, 0, 0] = st_scr[s]


def _retention(proj, decay_logit, seq_len, n_seq, state_f=None, state_b=None, emit_state=False):
    t = proj.shape[0]
    d = D_MODEL
    dk = RET_DK
    heads = RET_HEADS
    rows = n_seq * seq_len
    has_state = state_f is not None

    def col(offset):
        return pl.BlockSpec((rows, dk), lambda b, h: (b, offset * heads + h))

    in_specs = [pl.BlockSpec(memory_space=pltpu.SMEM)] + [col(o) for o in range(5)]
    args = [decay_logit, proj, proj, proj, proj, proj]
    if has_state:
        st = pl.BlockSpec((n_seq, 1, dk, dk), lambda b, h: (b, h, 0, 0))
        in_specs += [st, st]
        args += [state_f, state_b]
    out_specs = [pl.BlockSpec((rows, dk), lambda b, h: (b, h))]
    out_shape = [jax.ShapeDtypeStruct((t, d), BF16)]
    if emit_state:
        n_batch = t // seq_len
        st_out = pl.BlockSpec((n_seq, 1, 1, dk, dk), lambda b, h: (b, 0, h, 0, 0))
        out_specs += [st_out, st_out]
        out_shape += [jax.ShapeDtypeStruct((n_batch, 1, heads, dk, dk), F32)] * 2
    return pl.pallas_call(
        functools.partial(_ret_kernel, n_seq=n_seq, seq_len=seq_len,
                          has_state=has_state, emit_state=emit_state, unroll=4),
        grid=(t // rows, heads),
        in_specs=in_specs,
        out_specs=out_specs,
        out_shape=out_shape,
        scratch_shapes=[pltpu.VMEM((rows, dk), F32), pltpu.VMEM((rows, RET_CHUNK), BF16),
                        pltpu.VMEM((n_seq, dk, dk), F32)],
        compiler_params=_cparams(2),
        name="retention_state" if emit_state else "retention",
    )(*args)


def kernel(x_prompt, x_sample, cache_na_k, cache_na_v, state_ret_fwd, state_ret_bwd, c, c_ctx,
           w_ada, b_ada, g_norm, na_w_qkv, na_w_o, na_rpb, ret_w_in, ret_w_o, ret_decay_logit,
           ffn_w_in, ffn_w_out):
    bp, seq, d = x_prompt.shape
    bs, n_tok, _ = x_sample.shape
    past = cache_na_k.shape[2]
    xp = x_prompt.reshape(bp * seq, d)
    xs = x_sample.reshape(bs * n_tok, d)

    cond = jnp.concatenate([c_ctx[None, :], c, jnp.zeros((8 - 1 - bs, d), F32)], axis=0)
    mod = _ada_modulation(cond, w_ada, b_ada).reshape(w_ada.shape[0], 8, 6, d)
    mod_p = [mod[l, 0:1] for l in range(2)]
    mod_s = [mod[l, 1:1 + bs] for l in range(2)]

    w_qkv = na_w_qkv[0].astype(BF16)
    w_na_o = na_w_o[0].astype(BF16)
    w_ret_in = ret_w_in[0].astype(BF16)
    w_ret_o = ret_w_o[0].astype(BF16)
    w_ffn_in = ffn_w_in.astype(BF16)
    w_ffn_out = ffn_w_out.astype(BF16)

    qkv_p, k_p, v_p = _qkv_proj(xp, mod_p[0], g_norm[0], w_qkv, emit_kv=True)
    (qkv_s,) = _qkv_proj(xs, mod_s[0], g_norm[0], w_qkv, emit_kv=False)
    o_p = _ctx_attention(qkv_p, seq)
    bias = _na_bias_table(na_rpb[0], n_tok // GRID_W)
    ck = cache_na_k[:, 0].reshape(bs, past, d)
    cv = cache_na_v[:, 0].reshape(bs, past, d)
    o_s = _na_attention(qkv_s, ck, cv, bias, bs, n_tok)

    xp, hp = _mix_out(o_p, w_na_o, xp, mod_p[0], g_norm[0])
    xs, hs = _mix_out(o_s, w_na_o, xs, mod_s[0], g_norm[0])
    xp, hp = _ffn(hp, w_ffn_in[0], w_ffn_out[0], xp, mod_p[0], g_norm[0], mod_p[1], g_norm[1])
    xs, hs = _ffn(hs, w_ffn_in[0], w_ffn_out[0], xs, mod_s[0], g_norm[0], mod_s[1], g_norm[1])

    proj_p = _ret_in_proj(hp, w_ret_in)
    proj_s = _ret_in_proj(hs, w_ret_in, _rope_tables(n_tok, RET_DK), n_tok)
    y_p, sf, sb = _retention(proj_p, ret_decay_logit[0], seq, 8, emit_state=True)
    (y_s,) = _retention(proj_s, ret_decay_logit[0], n_tok, 1,
                        state_f=state_ret_fwd[:, 0], state_b=state_ret_bwd[:, 0])

    xp, hp = _mix_out(y_p, w_ret_o, xp, mod_p[1], g_norm[1])
    xs, hs = _mix_out(y_s, w_ret_o, xs, mod_s[1], g_norm[1])
    (xp,) = _ffn(hp, w_ffn_in[1], w_ffn_out[1], xp, mod_p[1], g_norm[1])
    (xs,) = _ffn(hs, w_ffn_in[1], w_ffn_out[1], xs, mod_s[1], g_norm[1])

    kv_shape = (bp, 1, seq, NA_HEADS, NA_HEAD_DIM)
    return (xp.reshape(bp, seq, d), xs.reshape(bs, n_tok, d),
            k_p.reshape(kv_shape), v_p.reshape(kv_shape),
            sf.astype(x_prompt.dtype), sb.astype(x_prompt.dtype))
```

```python
import functools

import jax
import jax.numpy as jnp
from jax import lax
from jax.experimental import pallas as pl
from jax.experimental.pallas import tpu as pltpu

D_MODEL = 2048
GRID_W = 64
NA_HEADS = 16
NA_HEAD_DIM = D_MODEL // NA_HEADS
NA_KH = 8
NA_KW = 16
RET_HEADS = 8
RET_DK = D_MODEL // RET_HEADS
RET_CHUNK = 128
FFN_HIDDEN = -(-8 * D_MODEL // (3 * 256)) * 256
ROPE_BASE = 10000.0
EPS = 1e-6

F32 = jnp.float32
BF16 = jnp.bfloat16
MASK_VALUE = -1e30

VMEM_LIMIT_BYTES = 56 * 1024 * 1024
ROW_CHUNK = 256
NA_QROWS = 4
NA_QBLK = NA_QROWS * GRID_W
NA_BAND_BLKS = 3


def _cparams(n_axes):
    return pltpu.CompilerParams(
        dimension_semantics=("arbitrary",) * n_axes,
        vmem_limit_bytes=VMEM_LIMIT_BYTES)


def _silu(x):
    return x / (1.0 + jnp.exp(-x))


def _rms(x, g):
    ms = jnp.mean(x * x, axis=-1, keepdims=True)
    return x * lax.rsqrt(ms + EPS) * g


def _dot(a, b):
    return jnp.dot(a, b, preferred_element_type=F32)


def _dot_nt(a, b):
    return lax.dot_general(a, b, (((1,), (1,)), ((), ())), preferred_element_type=F32)


def _dot_tn(a, b):
    return lax.dot_general(a, b, (((0,), (0,)), ((), ())), preferred_element_type=F32)


def _ada_kernel(cond_ref, w_ref, b_ref, o_ref):
    a = _silu(cond_ref[...]).astype(BF16)
    o_ref[0] = _dot(a, w_ref[0].astype(BF16)) + b_ref[0]


def _ada_modulation(cond, w_ada, b_ada, tn=1024):
    depth, d, n = w_ada.shape
    rows = cond.shape[0]
    return pl.pallas_call(
        _ada_kernel,
        grid=(depth, n // tn),
        in_specs=[
            pl.BlockSpec((rows, d), lambda l, j: (0, 0)),
            pl.BlockSpec((1, d, tn), lambda l, j: (l, 0, j)),
            pl.BlockSpec((1, 1, tn), lambda l, j: (l, 0, j)),
        ],
        out_specs=pl.BlockSpec((1, rows, tn), lambda l, j: (l, 0, j)),
        out_shape=jax.ShapeDtypeStruct((depth, rows, n), F32),
        compiler_params=_cparams(2),
        name="ada_modulation",
    )(cond, w_ada, b_ada.reshape(depth, 1, n))


def _row_chunks(n_rows):
    return [slice(r * ROW_CHUNK, (r + 1) * ROW_CHUNK) for r in range(n_rows // ROW_CHUNK)]


def _qkv_kernel(x_ref, mod_ref, g_ref, w_ref, *rest, nq, nk, emit_kv):
    if emit_kv:
        o_ref, k_ref, v_ref, h_scr = rest
    else:
        o_ref, h_scr = rest
    j = pl.program_id(1)
    chunks = _row_chunks(x_ref.shape[0])
    q_scale = NA_HEAD_DIM ** -0.5

    @pl.when(j == 0)
    def _():
        g, sc, sh = g_ref[0:1, :], mod_ref[0, 1:2, :], mod_ref[0, 0:1, :]
        for sl in chunks:
            h = (_rms(x_ref[sl, :], g) * (1.0 + sc) + sh).astype(BF16)
            h_scr[sl, :] = h
            o_ref[sl, :] = (_dot(h, w_ref[...]) * q_scale).astype(BF16)

    @pl.when((j > 0) & (j < nq))
    def _():
        for sl in chunks:
            o_ref[sl, :] = (_dot(h_scr[sl, :], w_ref[...]) * q_scale).astype(BF16)

    @pl.when((j >= nq) & (j < nq + nk))
    def _():
        for sl in chunks:
            acc = _dot(h_scr[sl, :], w_ref[...])
            o_ref[sl, :] = acc.astype(BF16)
            if emit_kv:
                k_ref[sl, :] = acc

    @pl.when(j >= nq + nk)
    def _():
        for sl in chunks:
            acc = _dot(h_scr[sl, :], w_ref[...])
            o_ref[sl, :] = acc.astype(BF16)
            if emit_kv:
                v_ref[sl, :] = acc


def _qkv_proj(x, mod, g, w, *, emit_kv, tn=1024):
    t, d = x.shape
    tm = 1024
    n = w.shape[1]
    groups = mod.shape[0]
    tiles_per_group = t // groups // tm
    nq = d // tn
    nk = d // tn
    out_shape = [jax.ShapeDtypeStruct((t, n), BF16)]
    out_specs = [pl.BlockSpec((tm, tn), lambda i, j: (i, j))]
    if emit_kv:
        out_shape += [jax.ShapeDtypeStruct((t, d), F32)] * 2
        out_specs += [
            pl.BlockSpec((tm, tn), lambda i, j: (i, jnp.clip(j - nq, 0, nk - 1))),
            pl.BlockSpec((tm, tn), lambda i, j: (i, jnp.clip(j - nq - nk, 0, nk - 1))),
        ]
    return pl.pallas_call(
        functools.partial(_qkv_kernel, nq=nq, nk=nk, emit_kv=emit_kv),
        grid=(t // tm, n // tn),
        in_specs=[
            pl.BlockSpec((tm, d), lambda i, j: (i, 0)),
            pl.BlockSpec((1, 6, d), lambda i, j: (i // tiles_per_group, 0, 0)),
            pl.BlockSpec((4, d), lambda i, j: (0, 0)),
            pl.BlockSpec((d, tn), lambda i, j: (0, j)),
        ],
        out_specs=out_specs,
        out_shape=out_shape,
        scratch_shapes=[pltpu.VMEM((tm, d), BF16)],
        compiler_params=_cparams(2),
        name="qkv_proj_kv" if emit_kv else "qkv_proj",
    )(x, mod, g, w)


def _ctx_attn_kernel(q_ref, k_ref, v_ref, o_ref):
    for h in range(NA_HEADS):
        sl = slice(h * NA_HEAD_DIM, (h + 1) * NA_HEAD_DIM)
        s = _dot_nt(q_ref[:, sl], k_ref[:, sl])
        m = jnp.max(s, axis=-1, keepdims=True)
        p = jnp.exp(s - m)
        l = jnp.sum(p, axis=-1, keepdims=True)
        o = _dot(p.astype(BF16), v_ref[:, sl])
        o_ref[:, sl] = (o * (1.0 / l)).astype(BF16)


def _ctx_attention(qkv, seq):
    t = qkv.shape[0]
    d = D_MODEL
    return pl.pallas_call(
        _ctx_attn_kernel,
        grid=(t // seq,),
        in_specs=[
            pl.BlockSpec((seq, d), lambda b: (b, 0)),
            pl.BlockSpec((seq, d), lambda b: (b, 1)),
            pl.BlockSpec((seq, d), lambda b: (b, 2)),
        ],
        out_specs=pl.BlockSpec((seq, d), lambda b: (b, 0)),
        out_shape=jax.ShapeDtypeStruct((t, d), BF16),
        compiler_params=_cparams(1),
        name="ctx_attention",
    )(qkv, qkv, qkv)


def _band_start_blk(rb, n_rb):
    return jnp.clip(rb - 1, 0, n_rb - NA_BAND_BLKS)


def _bias_tile_plan(n_grid_rows):
    n_rb = n_grid_rows // NA_QROWS
    plans = []
    for rb in (0, 1, n_rb - 1):
        r0 = rb * NA_QROWS
        a = min(max(rb - 1, 0), n_rb - NA_BAND_BLKS) * NA_QROWS
        rows = []
        for j in range(NA_BAND_BLKS * NA_QROWS):
            kr = a + j
            pairs = []
            for ip in range(NA_QROWS // 2):
                drs = []
                for i in (2 * ip, 2 * ip + 1):
                    r = r0 + i
                    rs = min(max(r - NA_KH // 2, 0), n_grid_rows - NA_KH)
                    drs.append(kr - r + NA_KH - 1 if rs <= kr < rs + NA_KH else None)
                pairs.append(tuple(drs))
            rows.append(pairs)
        plans.append(rows)
    return plans


def _na_bias_kernel(rpb_ref, o_ref, *, plan):
    h = pl.program_id(0)
    n_dr = 2 * NA_KH - 1
    n_dc = 2 * NA_KW - 1
    shape = (GRID_W, 2 * GRID_W)
    kc = lax.broadcasted_iota(jnp.int32, shape, 0)
    lane = lax.broadcasted_iota(jnp.int32, shape, 1)
    qc = lane & (GRID_W - 1)
    right = lane >= GRID_W
    diff = kc - qc
    cstart = jnp.clip(qc - NA_KW // 2, 0, GRID_W - NA_KW)
    in_window = (kc >= cstart) & (kc < cstart + NA_KW)
    masked = jnp.full(shape, MASK_VALUE, F32)

    used = sorted({dr for rows in plan for pairs in rows for pr in pairs for dr in pr if dr is not None})
    tiles = {}
    for dr in used:
        t = masked
        for dc in range(n_dc):
            val = rpb_ref[h * (n_dr * n_dc) + dr * n_dc + dc]
            t = jnp.where(diff == dc - (NA_KW - 1), val, t)
        tiles[dr] = jnp.where(in_window, t, MASK_VALUE)
    tiles[None] = masked

    for ty, rows in enumerate(plan):
        for j, pairs in enumerate(rows):
            for ip, (dl, dr) in enumerate(pairs):
                tile = tiles[dl] if dl == dr else jnp.where(right, tiles[dr], tiles[dl])
                o_ref[ty, 0, j * GRID_W:(j + 1) * GRID_W, ip * 2 * GRID_W:(ip + 1) * 2 * GRID_W] = tile


def _na_bias_table(rpb, n_grid_rows):
    heads = rpb.shape[0]
    band = NA_BAND_BLKS * NA_QBLK
    plan = _bias_tile_plan(n_grid_rows)
    return pl.pallas_call(
        functools.partial(_na_bias_kernel, plan=plan),
        grid=(heads,),
        in_specs=[pl.BlockSpec(memory_space=pltpu.SMEM)],
        out_specs=pl.BlockSpec((3, 1, band, NA_QBLK), lambda h: (0, h, 0, 0)),
        out_shape=jax.ShapeDtypeStruct((3, heads, band, NA_QBLK), F32),
        compiler_params=_cparams(1),
        name="na_bias_table",
    )(rpb.reshape(-1))


def _na_attn_kernel(q_ref, k0_ref, k1_ref, k2_ref, v0_ref, v1_ref, v2_ref,
                    ck_ref, cv_ref, bias_ref, o_ref):
    dh = NA_HEAD_DIM
    for hh in range(q_ref.shape[1] // dh):
        sl = slice(hh * dh, (hh + 1) * dh)
        q = q_ref[:, sl]
        keys = [k0_ref[:, sl], k1_ref[:, sl], k2_ref[:, sl], ck_ref[0, :, sl].astype(BF16)]
        vals = [v0_ref[:, sl], v1_ref[:, sl], v2_ref[:, sl], cv_ref[0, :, sl].astype(BF16)]
        s = [_dot_nt(kj, q) for kj in keys]
        for jb in range(NA_BAND_BLKS):
            s[jb] = s[jb] + bias_ref[0, hh, jb * NA_QBLK:(jb + 1) * NA_QBLK, :]
        m = s[0].max(axis=0, keepdims=True)
        for sj in s[1:]:
            m = jnp.maximum(m, sj.max(axis=0, keepdims=True))
        p = [jnp.exp(sj - m) for sj in s]
        l = p[0].sum(axis=0, keepdims=True)
        for pj in p[1:]:
            l = l + pj.sum(axis=0, keepdims=True)
        o_t = _dot_tn(vals[0], p[0].astype(BF16))
        for vj, pj in zip(vals[1:], p[1:]):
            o_t = o_t + _dot_tn(vj, pj.astype(BF16))
        o_ref[:, sl] = (o_t * (1.0 / l)).T.astype(BF16)


def _na_attention(qkv, ck, cv, bias, batch, n_tok, heads_per_step=4):
    d = D_MODEL
    width = heads_per_step * NA_HEAD_DIM
    n_hg = NA_HEADS // heads_per_step
    n_rb = n_tok // NA_QBLK
    past = ck.shape[1]

    def qmap(hg, b, rb):
        return (b * n_rb + rb, hg)

    def band_map(col0, off):
        def f(hg, b, rb):
            return (b * n_rb + _band_start_blk(rb, n_rb) + off, col0 + hg)
        return f

    def bias_map(hg, b, rb):
        ty = jnp.where(rb == 0, 0, jnp.where(rb == n_rb - 1, 2, 1))
        return (ty, hg, 0, 0)

    blk = pl.BlockSpec((NA_QBLK, width), qmap)
    in_specs = [blk]
    in_specs += [pl.BlockSpec((NA_QBLK, width), band_map(n_hg, off)) for off in range(NA_BAND_BLKS)]
    in_specs += [pl.BlockSpec((NA_QBLK, width), band_map(2 * n_hg, off)) for off in range(NA_BAND_BLKS)]
    in_specs += [pl.BlockSpec((1, past, width), lambda hg, b, rb: (b, 0, hg))] * 2
    in_specs += [pl.BlockSpec((1, heads_per_step, NA_BAND_BLKS * NA_QBLK, NA_QBLK), bias_map)]
    return pl.pallas_call(
        _na_attn_kernel,
        grid=(n_hg, batch, n_rb),
        in_specs=in_specs,
        out_specs=blk,
        out_shape=jax.ShapeDtypeStruct((batch * n_tok, d), BF16),
        compiler_params=_cparams(3),
        name="na_attention",
    )(qkv, qkv, qkv, qkv, qkv, qkv, qkv, ck, cv, bias)


def _mix_out_kernel(o_ref, w_ref, x_ref, mod_ref, g_ref, x1_ref, h2_ref):
    tm = o_ref.shape[0]
    ga = mod_ref[0, 2:3, :]
    sh = mod_ref[0, 3:4, :]
    sc = mod_ref[0, 4:5, :]
    for r in range(tm // ROW_CHUNK):
        sl = slice(r * ROW_CHUNK, (r + 1) * ROW_CHUNK)
        y = _dot(o_ref[sl, :], w_ref[...])
        x1 = x_ref[sl, :] + ga * _rms(y, g_ref[1:2, :])
        x1_ref[sl, :] = x1
        h2_ref[sl, :] = (_rms(x1, g_ref[2:3, :]) * (1.0 + sc) + sh).astype(BF16)


def _mix_out(o, w, x, mod, g, tm=512):
    t, d = x.shape
    groups = mod.shape[0]
    tiles_per_group = t // groups // tm
    return pl.pallas_call(
        _mix_out_kernel,
        grid=(t // tm,),
        in_specs=[
            pl.BlockSpec((tm, d), lambda i: (i, 0)),
            pl.BlockSpec((d, d), lambda i: (0, 0)),
            pl.BlockSpec((tm, d), lambda i: (i, 0)),
            pl.BlockSpec((1, 6, d), lambda i: (i // tiles_per_group, 0, 0)),
            pl.BlockSpec((4, d), lambda i: (0, 0)),
        ],
        out_specs=[pl.BlockSpec((tm, d), lambda i: (i, 0)),
                   pl.BlockSpec((tm, d), lambda i: (i, 0))],
        out_shape=[jax.ShapeDtypeStruct((t, d), F32), jax.ShapeDtypeStruct((t, d), BF16)],
        compiler_params=_cparams(1),
        name="mix_out",
    )(o, w, x, mod, g)


def _ffn_kernel(h_ref, wg_ref, wu_ref, wo_ref, x_ref, mod_ref, g_ref, *rest, emit_next):
    if emit_next:
        modn_ref, gn_ref, x2_ref, hn_ref = rest
    else:
        (x2_ref,) = rest
    c = pl.program_id(1)
    last = pl.num_programs(1) - 1
    row_chunks = _row_chunks(h_ref.shape[0])

    def partial_out(sl):
        h = h_ref[sl, :]
        a = (_silu(_dot(h, wg_ref[0])) * _dot(h, wu_ref[0])).astype(BF16)
        return _dot(a, wo_ref[0])

    @pl.when(c == 0)
    def _():
        for sl in row_chunks:
            x2_ref[sl, :] = partial_out(sl)

    @pl.when((c > 0) & (c < last))
    def _():
        for sl in row_chunks:
            x2_ref[sl, :] += partial_out(sl)

    @pl.when(c == last)
    def _():
        ga = mod_ref[0, 5:6, :]
        for sl in row_chunks:
            y = x2_ref[sl, :] + partial_out(sl)
            x2 = x_ref[sl, :] + ga * _rms(y, g_ref[3:4, :])
            x2_ref[sl, :] = x2
            if emit_next:
                hn = _rms(x2, gn_ref[0:1, :]) * (1.0 + modn_ref[0, 1:2, :]) + modn_ref[0, 0:1, :]
                hn_ref[sl, :] = hn.astype(BF16)


def _ffn(h, w_in, w_out, layer, x, mod, g, mod_next=None, g_next=None, tm=1024, tf=512):
    t, d = x.shape
    hidden = w_out.shape[1]
    n_c = hidden // tf
    groups = mod.shape[0]
    tiles_per_group = t // groups // tm
    emit_next = mod_next is not None
    row = pl.BlockSpec((tm, d), lambda i, c: (i, 0))
    x_spec = pl.BlockSpec((tm, d), lambda i, c: (i, 0), pipeline_mode=pl.Buffered(1)) if emit_next else row
    mod_spec = pl.BlockSpec((1, 6, d), lambda i, c: (i // tiles_per_group, 0, 0))
    g_spec = pl.BlockSpec((4, d), lambda i, c: (0, 0))
    in_specs = [
        row,
        pl.BlockSpec((1, d, tf), lambda i, c: (layer, 0, c)),
        pl.BlockSpec((1, d, tf), lambda i, c: (layer, 0, n_c + c)),
        pl.BlockSpec((1, tf, d), lambda i, c: (layer, c, 0)),
        x_spec, mod_spec, g_spec,
    ]
    args = [h, w_in, w_in, w_out, x, mod, g]
    out_specs = [row]
    out_shape = [jax.ShapeDtypeStruct((t, d), F32)]
    if emit_next:
        in_specs += [mod_spec, g_spec]
        args += [mod_next, g_next]
        out_specs.append(row)
        out_shape.append(jax.ShapeDtypeStruct((t, d), BF16))
    return pl.pallas_call(
        functools.partial(_ffn_kernel, emit_next=emit_next),
        grid=(t // tm, n_c),
        in_specs=in_specs,
        out_specs=out_specs,
        out_shape=out_shape,
        compiler_params=_cparams(2),
        name="ffn_next" if emit_next else "ffn",
    )(*args)


def _ret_in_kernel(h_ref, w_ref, *rest, rope, n_qk, n_q, n_v):
    if rope:
        cos_ref, sin_ref, o_ref = rest
    else:
        (o_ref,) = rest
    j = pl.program_id(1)
    tn = w_ref.shape[1]
    half = RET_DK // 2
    chunks = _row_chunks(h_ref.shape[0])

    @pl.when(j < n_qk)
    def _():
        scale = jnp.where(j < n_q, RET_DK ** -0.5, 1.0).astype(F32)
        for sl in chunks:
            acc = _dot(h_ref[sl, :], w_ref[...])
            for hh in range(tn // RET_DK):
                x = acc[:, hh * RET_DK:(hh + 1) * RET_DK]
                if rope:
                    rot = jnp.concatenate(
                        [pltpu.roll(x[:, :half], half // 2, 1), pltpu.roll(x[:, half:], half // 2, 1)], axis=1)
                    x = x * cos_ref[sl, :] + rot * sin_ref[sl, :]
                o_ref[sl, hh * RET_DK:(hh + 1) * RET_DK] = (x * scale).astype(BF16)

    @pl.when((j >= n_qk) & (j < n_qk + n_v))
    def _():
        for sl in chunks:
            o_ref[sl, :] = _dot(h_ref[sl, :], w_ref[...]).astype(BF16)

    @pl.when(j >= n_qk + n_v)
    def _():
        for sl in chunks:
            o_ref[sl, :] = _silu(_dot(h_ref[sl, :], w_ref[...])).astype(BF16)


def _ret_in_proj(h, w, rope_tables=None, seq_len=None, tm=1024, tn=1024):
    t, d = h.shape
    n = w.shape[1]
    rope = rope_tables is not None
    in_specs = [
        pl.BlockSpec((tm, d), lambda i, j: (i, 0)),
        pl.BlockSpec((d, tn), lambda i, j: (0, j)),
    ]
    args = [h, w]
    if rope:
        tiles_per_seq = seq_len // tm
        tab = pl.BlockSpec((tm, RET_DK), lambda i, j: (i % tiles_per_seq, 0))
        in_specs += [tab, tab]
        args += list(rope_tables)
    return pl.pallas_call(
        functools.partial(_ret_in_kernel, rope=rope, n_qk=2 * d // tn, n_q=d // tn, n_v=d // tn),
        grid=(t // tm, n // tn),
        in_specs=in_specs,
        out_specs=pl.BlockSpec((tm, tn), lambda i, j: (i, j)),
        out_shape=jax.ShapeDtypeStruct((t, n), BF16),
        compiler_params=_cparams(2),
        name="ret_in_rope" if rope else "ret_in",
    )(*args)


def _rope_tables(n_tokens, dim):
    tok = jnp.arange(n_tokens)
    rows = (tok // GRID_W).astype(F32)
    cols = (tok % GRID_W).astype(F32)
    half = dim // 2
    inv = jnp.power(ROPE_BASE, -jnp.arange(0, half, 2, dtype=F32) / half)
    ar = rows[:, None] * inv[None, :]
    ac = cols[:, None] * inv[None, :]
    ang = jnp.concatenate([ar, ar, ac, ac], axis=-1)
    quarter = dim // 4
    sign = jnp.where((jnp.arange(dim) // quarter) % 2 == 0, -1.0, 1.0).astype(F32)
    return jnp.cos(ang), jnp.sin(ang) * sign[None, :]


def _log_sigmoid(x):
    y = -x
    return -(jnp.maximum(y, 0.0) + jnp.log1p(jnp.exp(-jnp.abs(y))))


def _group_norm(o):
    mu = jnp.mean(o, axis=-1, keepdims=True)
    dev = o - mu
    var = jnp.mean(dev * dev, axis=-1, keepdims=True)
    return dev * lax.rsqrt(var + EPS)


def _ret_kernel(dl_ref, q_ref, k_ref, v_ref, gf_ref, gb_ref, *rest,
                n_seq, seq_len, has_state, emit_state, unroll):
    rest = list(rest)
    if has_state:
        s0f_ref, s0b_ref = rest[:2]
        rest = rest[2:]
    y_ref = rest.pop(0)
    if emit_state:
        sf_out_ref, sb_out_ref = rest[:2]
        rest = rest[2:]
    of_scr, pb_scr, st_scr = rest

    head = pl.program_id(1)
    C = RET_CHUNK
    dk = RET_DK
    n_chunks = seq_len // C

    def lg(shape, direction):
        return _log_sigmoid(jnp.full(shape, dl_ref[direction, head], F32))

    ii = lax.broadcasted_iota(jnp.int32, (C, C), 0)
    jj = lax.broadcasted_iota(jnp.int32, (C, C), 1)
    causal = ii >= jj
    anti = jj >= ii
    dist = jnp.abs(ii - jj).astype(F32)
    decay_f = jnp.where(causal, jnp.exp(jnp.where(causal, dist, 0.0) * lg((C, C), 0)), 0.0)
    decay_b = jnp.where(anti, jnp.exp(jnp.where(anti, dist, 0.0) * lg((C, C), 1)), 0.0)
    pos = lax.broadcasted_iota(jnp.int32, (C, dk), 0).astype(F32)
    lgf = lg((C, dk), 0)
    lgb = lg((C, dk), 1)
    qd_f = jnp.exp((pos + 1.0) * lgf)
    kd_f = jnp.exp((C - 1.0 - pos) * lgf)
    qd_b = jnp.exp((C - pos) * lgb)
    kd_b = jnp.exp(pos * lgb)
    cd_f = jnp.exp(C * lg((1, dk), 0))
    cd_b = jnp.exp(C * lg((1, dk), 1))

    def chunk_rows(s, t):
        return pl.ds(pl.multiple_of(s * seq_len + t * C, C), C)

    def advance(s, q, k, v, scores, qd, kd, cd):
        state = st_scr[s]
        o = _dot(scores, v) + _dot(q, state.astype(BF16)) * qd
        st_scr[s] = state * cd + _dot_tn((k.astype(F32) * kd).astype(BF16), v)
        return o

    def fwd_chunk(s, t):
        rows = chunk_rows(s, t)
        q, k, v = q_ref[rows, :], k_ref[rows, :], v_ref[rows, :]
        raw = _dot_nt(q, k)
        pb_scr[rows, :] = (raw * decay_b).astype(BF16)
        o = advance(s, q, k, v, (raw * decay_f).astype(BF16), qd_f, kd_f, cd_f)
        of_scr[rows, :] = _group_norm(o)

    def bwd_chunk(s, t):
        rows = chunk_rows(s, t)
        q, k, v = q_ref[rows, :], k_ref[rows, :], v_ref[rows, :]
        o = advance(s, q, k, v, pb_scr[rows, :], qd_b, kd_b, cd_b)
        y = (gf_ref[rows, :].astype(F32) * of_scr[rows, :]
             + gb_ref[rows, :].astype(F32) * _group_norm(o))
        y_ref[rows, :] = y.astype(BF16)

    def sweep(chunk_fn, reverse):
        def at(t):
            for s in range(n_seq):
                chunk_fn(s, n_chunks - 1 - t if reverse else t)

        if n_chunks <= 4:
            for t in range(n_chunks):
                at(t)
        else:
            lax.fori_loop(0, n_chunks, lambda t, carry: (at(t), carry)[1], 0, unroll=unroll)

    for s in range(n_seq):
        st_scr[s] = s0f_ref[s, 0] if has_state else jnp.zeros((dk, dk), F32)
    sweep(fwd_chunk, reverse=False)
    for s in range(n_seq):
        if emit_state:
            sf_out_ref[s, 0, 0] = st_scr[s]
        st_scr[s] = s0b_ref[s, 0] if has_state else jnp.zeros((dk, dk), F32)
    sweep(bwd_chunk, reverse=True)
    if emit_state:
        for s in range(n_seq):
            sb_out_ref[s, 0, 0] = st_scr[s]


def _retention(proj, decay_logit, seq_len, n_seq, state_f=None, state_b=None, emit_state=False):
    t = proj.shape[0]
    d = D_MODEL
    dk = RET_DK
    heads = RET_HEADS
    rows = n_seq * seq_len
    has_state = state_f is not None

    def col(offset):
        return pl.BlockSpec((rows, dk), lambda b, h: (b, offset * heads + h))

    in_specs = [pl.BlockSpec(memory_space=pltpu.SMEM)] + [col(o) for o in range(5)]
    args = [decay_logit, proj, proj, proj, proj, proj]
    if has_state:
        st = pl.BlockSpec((n_seq, 1, dk, dk), lambda b, h: (b, h, 0, 0))
        in_specs += [st, st]
        args += [state_f, state_b]
    out_specs = [pl.BlockSpec((rows, dk), lambda b, h: (b, h))]
    out_shape = [jax.ShapeDtypeStruct((t, d), BF16)]
    if emit_state:
        n_batch = t // seq_len
        st_out = pl.BlockSpec((n_seq, 1, 1, dk, dk), lambda b, h: (b, 0, h, 0, 0))
        out_specs += [st_out, st_out]
        out_shape += [jax.ShapeDtypeStruct((n_batch, 1, heads, dk, dk), F32)] * 2
    return pl.pallas_call(
        functools.partial(_ret_kernel, n_seq=n_seq, seq_len=seq_len,
                          has_state=has_state, emit_state=emit_state, unroll=4),
        grid=(t // rows, heads),
        in_specs=in_specs,
        out_specs=out_specs,
        out_shape=out_shape,
        scratch_shapes=[pltpu.VMEM((rows, dk), F32), pltpu.VMEM((rows, RET_CHUNK), BF16),
                        pltpu.VMEM((n_seq, dk, dk), F32)],
        compiler_params=_cparams(2),
        name="retention_state" if emit_state else "retention",
    )(*args)


def kernel(x_prompt, x_sample, cache_na_k, cache_na_v, state_ret_fwd, state_ret_bwd, c, c_ctx,
           w_ada, b_ada, g_norm, na_w_qkv, na_w_o, na_rpb, ret_w_in, ret_w_o, ret_decay_logit,
           ffn_w_in, ffn_w_out):
    bp, seq, d = x_prompt.shape
    bs, n_tok, _ = x_sample.shape
    past = cache_na_k.shape[2]
    xp = x_prompt.reshape(bp * seq, d)
    xs = x_sample.reshape(bs * n_tok, d)

    cond = jnp.concatenate([c_ctx[None, :], c, jnp.zeros((8 - 1 - bs, d), F32)], axis=0)
    mod = _ada_modulation(cond, w_ada, b_ada).reshape(w_ada.shape[0], 8, 6, d)
    mod_p = [mod[l, 0:1] for l in range(2)]
    mod_s = [mod[l, 1:1 + bs] for l in range(2)]

    w_qkv = na_w_qkv[0].astype(BF16)
    w_na_o = na_w_o[0].astype(BF16)
    w_ret_in = ret_w_in[0].astype(BF16)
    w_ret_o = ret_w_o[0].astype(BF16)
    w_ffn_in = ffn_w_in.astype(BF16)
    w_ffn_out = ffn_w_out.astype(BF16)

    qkv_p, k_p, v_p = _qkv_proj(xp, mod_p[0], g_norm[0], w_qkv, emit_kv=True)
    (qkv_s,) = _qkv_proj(xs, mod_s[0], g_norm[0], w_qkv, emit_kv=False)
    o_p = _ctx_attention(qkv_p, seq)
    bias = _na_bias_table(na_rpb[0], n_tok // GRID_W)
    ck = cache_na_k[:, 0].reshape(bs, past, d)
    cv = cache_na_v[:, 0].reshape(bs, past, d)
    o_s = _na_attention(qkv_s, ck, cv, bias, bs, n_tok)

    xp, hp = _mix_out(o_p, w_na_o, xp, mod_p[0], g_norm[0])
    xs, hs = _mix_out(o_s, w_na_o, xs, mod_s[0], g_norm[0])
    xp, hp = _ffn(hp, w_ffn_in, w_ffn_out, 0, xp, mod_p[0], g_norm[0], mod_p[1], g_norm[1])
    xs, hs = _ffn(hs, w_ffn_in, w_ffn_out, 0, xs, mod_s[0], g_norm[0], mod_s[1], g_norm[1])

    proj_p = _ret_in_proj(hp, w_ret_in)
    proj_s = _ret_in_proj(hs, w_ret_in, _rope_tables(n_tok, RET_DK), n_tok)
    y_p, sf, sb = _retention(proj_p, ret_decay_logit[0], seq, 8, emit_state=True)
    (y_s,) = _retention(proj_s, ret_decay_logit[0], n_tok, 1,
                        state_f=state_ret_fwd[:, 0], state_b=state_ret_bwd[:, 0])

    xp, hp = _mix_out(y_p, w_ret_o, xp, mod_p[1], g_norm[1])
    xs, hs = _mix_out(y_s, w_ret_o, xs, mod_s[1], g_norm[1])
    (xp,) = _ffn(hp, w_ffn_in, w_ffn_out, 1, xp, mod_p[1], g_norm[1])
    (xs,) = _ffn(hs, w_ffn_in, w_ffn_out, 1, xs, mod_s[1], g_norm[1])

    kv_shape = (bp, 1, seq, NA_HEADS, NA_HEAD_DIM)
    return (xp.reshape(bp, seq, d), xs.reshape(bs, n_tok, d),
            k_p.reshape(kv_shape), v_p.reshape(kv_shape),
            sf.astype(x_prompt.dtype), sb.astype(x_prompt.dtype))
```

```python
import functools

import jax
import jax.numpy as jnp
from jax import lax
from jax.experimental import pallas as pl
from jax.experimental.pallas import tpu as pltpu

D_MODEL = 2048
GRID_W = 64
NA_HEADS = 16
NA_HEAD_DIM = D_MODEL // NA_HEADS
NA_KH = 8
NA_KW = 16
RET_HEADS = 8
RET_DK = D_MODEL // RET_HEADS
RET_CHUNK = 128
FFN_HIDDEN = -(-8 * D_MODEL // (3 * 256)) * 256
ROPE_BASE = 10000.0
EPS = 1e-6

F32 = jnp.float32
BF16 = jnp.bfloat16
MASK_VALUE = -1e30

VMEM_LIMIT_BYTES = 56 * 1024 * 1024
ROW_CHUNK = 256
NA_QROWS = 4
NA_QBLK = NA_QROWS * GRID_W
NA_BAND_BLKS = 3


def _cparams(n_axes):
    return pltpu.CompilerParams(
        dimension_semantics=("arbitrary",) * n_axes,
        vmem_limit_bytes=VMEM_LIMIT_BYTES)


def _silu(x):
    return x / (1.0 + jnp.exp(-x))


def _rms(x, g):
    ms = jnp.mean(x * x, axis=-1, keepdims=True)
    return x * lax.rsqrt(ms + EPS) * g


def _dot(a, b):
    return jnp.dot(a, b, preferred_element_type=F32)


def _dot_nt(a, b):
    return lax.dot_general(a, b, (((1,), (1,)), ((), ())), preferred_element_type=F32)


def _dot_tn(a, b):
    return lax.dot_general(a, b, (((0,), (0,)), ((), ())), preferred_element_type=F32)


def _ada_kernel(cond_ref, w_ref, b_ref, o_ref):
    a = _silu(cond_ref[...]).astype(BF16)
    o_ref[0] = _dot(a, w_ref[0].astype(BF16)) + b_ref[0]


def _ada_modulation(cond, w_ada, b_ada, tn=1024):
    depth, d, n = w_ada.shape
    rows = cond.shape[0]
    return pl.pallas_call(
        _ada_kernel,
        grid=(depth, n // tn),
        in_specs=[
            pl.BlockSpec((rows, d), lambda l, j: (0, 0)),
            pl.BlockSpec((1, d, tn), lambda l, j: (l, 0, j)),
            pl.BlockSpec((1, 1, tn), lambda l, j: (l, 0, j)),
        ],
        out_specs=pl.BlockSpec((1, rows, tn), lambda l, j: (l, 0, j)),
        out_shape=jax.ShapeDtypeStruct((depth, rows, n), F32),
        compiler_params=_cparams(2),
        name="ada_modulation",
    )(cond, w_ada, b_ada.reshape(depth, 1, n))


def _row_chunks(n_rows):
    return [slice(r * ROW_CHUNK, (r + 1) * ROW_CHUNK) for r in range(n_rows // ROW_CHUNK)]


def _qkv_kernel(x_ref, mod_ref, g_ref, w_ref, *rest, nq, nk, emit_kv):
    if emit_kv:
        o_ref, k_ref, v_ref, h_scr = rest
    else:
        o_ref, h_scr = rest
    j = pl.program_id(1)
    chunks = _row_chunks(x_ref.shape[0])
    q_scale = NA_HEAD_DIM ** -0.5

    @pl.when(j == 0)
    def _():
        g, sc, sh = g_ref[0:1, :], mod_ref[0, 1:2, :], mod_ref[0, 0:1, :]
        for sl in chunks:
            h = (_rms(x_ref[sl, :], g) * (1.0 + sc) + sh).astype(BF16)
            h_scr[sl, :] = h
            o_ref[sl, :] = (_dot(h, w_ref[...]) * q_scale).astype(BF16)

    @pl.when((j > 0) & (j < nq))
    def _():
        for sl in chunks:
            o_ref[sl, :] = (_dot(h_scr[sl, :], w_ref[...]) * q_scale).astype(BF16)

    @pl.when((j >= nq) & (j < nq + nk))
    def _():
        for sl in chunks:
            acc = _dot(h_scr[sl, :], w_ref[...])
            o_ref[sl, :] = acc.astype(BF16)
            if emit_kv:
                k_ref[sl, :] = acc

    @pl.when(j >= nq + nk)
    def _():
        for sl in chunks:
            acc = _dot(h_scr[sl, :], w_ref[...])
            o_ref[sl, :] = acc.astype(BF16)
            if emit_kv:
                v_ref[sl, :] = acc


def _qkv_proj(x, mod, g, w, *, emit_kv, tn=1024):
    t, d = x.shape
    tm = 1024
    n = w.shape[1]
    groups = mod.shape[0]
    tiles_per_group = t // groups // tm
    nq = d // tn
    nk = d // tn
    out_shape = [jax.ShapeDtypeStruct((t, n), BF16)]
    out_specs = [pl.BlockSpec((tm, tn), lambda i, j: (i, j))]
    if emit_kv:
        out_shape += [jax.ShapeDtypeStruct((t, d), F32)] * 2
        out_specs += [
            pl.BlockSpec((tm, tn), lambda i, j: (i, jnp.clip(j - nq, 0, nk - 1))),
            pl.BlockSpec((tm, tn), lambda i, j: (i, jnp.clip(j - nq - nk, 0, nk - 1))),
        ]
    return pl.pallas_call(
        functools.partial(_qkv_kernel, nq=nq, nk=nk, emit_kv=emit_kv),
        grid=(t // tm, n // tn),
        in_specs=[
            pl.BlockSpec((tm, d), lambda i, j: (i, 0)),
            pl.BlockSpec((1, 6, d), lambda i, j: (i // tiles_per_group, 0, 0)),
            pl.BlockSpec((4, d), lambda i, j: (0, 0)),
            pl.BlockSpec((d, tn), lambda i, j: (0, j)),
        ],
        out_specs=out_specs,
        out_shape=out_shape,
        scratch_shapes=[pltpu.VMEM((tm, d), BF16)],
        compiler_params=_cparams(2),
        name="qkv_proj_kv" if emit_kv else "qkv_proj",
    )(x, mod, g, w)


def _ctx_attn_kernel(q_ref, k_ref, v_ref, o_ref):
    for h in range(NA_HEADS):
        sl = slice(h * NA_HEAD_DIM, (h + 1) * NA_HEAD_DIM)
        s = _dot_nt(q_ref[:, sl], k_ref[:, sl])
        m = jnp.max(s, axis=-1, keepdims=True)
        p = jnp.exp(s - m)
        l = jnp.sum(p, axis=-1, keepdims=True)
        o = _dot(p.astype(BF16), v_ref[:, sl])
        o_ref[:, sl] = (o * (1.0 / l)).astype(BF16)


def _ctx_attention(qkv, seq):
    t = qkv.shape[0]
    d = D_MODEL
    return pl.pallas_call(
        _ctx_attn_kernel,
        grid=(t // seq,),
        in_specs=[
            pl.BlockSpec((seq, d), lambda b: (b, 0)),
            pl.BlockSpec((seq, d), lambda b: (b, 1)),
            pl.BlockSpec((seq, d), lambda b: (b, 2)),
        ],
        out_specs=pl.BlockSpec((seq, d), lambda b: (b, 0)),
        out_shape=jax.ShapeDtypeStruct((t, d), BF16),
        compiler_params=_cparams(1),
        name="ctx_attention",
    )(qkv, qkv, qkv)


def _band_start_blk(rb, n_rb):
    return jnp.clip(rb - 1, 0, n_rb - NA_BAND_BLKS)


def _bias_tile_plan(n_grid_rows):
    n_rb = n_grid_rows // NA_QROWS
    plans = []
    for rb in (0, 1, n_rb - 1):
        r0 = rb * NA_QROWS
        a = min(max(rb - 1, 0), n_rb - NA_BAND_BLKS) * NA_QROWS
        rows = []
        for j in range(NA_BAND_BLKS * NA_QROWS):
            kr = a + j
            pairs = []
            for ip in range(NA_QROWS // 2):
                drs = []
                for i in (2 * ip, 2 * ip + 1):
                    r = r0 + i
                    rs = min(max(r - NA_KH // 2, 0), n_grid_rows - NA_KH)
                    drs.append(kr - r + NA_KH - 1 if rs <= kr < rs + NA_KH else None)
                pairs.append(tuple(drs))
            rows.append(pairs)
        plans.append(rows)
    return plans


def _na_bias_kernel(rpb_ref, o_ref, *, plan):
    h = pl.program_id(0)
    n_dr = 2 * NA_KH - 1
    n_dc = 2 * NA_KW - 1
    shape = (GRID_W, 2 * GRID_W)
    kc = lax.broadcasted_iota(jnp.int32, shape, 0)
    lane = lax.broadcasted_iota(jnp.int32, shape, 1)
    qc = lane & (GRID_W - 1)
    right = lane >= GRID_W
    diff = kc - qc
    cstart = jnp.clip(qc - NA_KW // 2, 0, GRID_W - NA_KW)
    in_window = (kc >= cstart) & (kc < cstart + NA_KW)
    masked = jnp.full(shape, MASK_VALUE, F32)

    used = sorted({dr for rows in plan for pairs in rows for pr in pairs for dr in pr if dr is not None})
    tiles = {}
    for dr in used:
        t = masked
        for dc in range(n_dc):
            val = rpb_ref[h * (n_dr * n_dc) + dr * n_dc + dc]
            t = jnp.where(diff == dc - (NA_KW - 1), val, t)
        tiles[dr] = jnp.where(in_window, t, MASK_VALUE)
    tiles[None] = masked

    for ty, rows in enumerate(plan):
        for j, pairs in enumerate(rows):
            for ip, (dl, dr) in enumerate(pairs):
                tile = tiles[dl] if dl == dr else jnp.where(right, tiles[dr], tiles[dl])
                o_ref[ty, 0, j * GRID_W:(j + 1) * GRID_W, ip * 2 * GRID_W:(ip + 1) * 2 * GRID_W] = tile


def _na_bias_table(rpb, n_grid_rows):
    heads = rpb.shape[0]
    band = NA_BAND_BLKS * NA_QBLK
    plan = _bias_tile_plan(n_grid_rows)
    return pl.pallas_call(
        functools.partial(_na_bias_kernel, plan=plan),
        grid=(heads,),
        in_specs=[pl.BlockSpec(memory_space=pltpu.SMEM)],
        out_specs=pl.BlockSpec((3, 1, band, NA_QBLK), lambda h: (0, h, 0, 0)),
        out_shape=jax.ShapeDtypeStruct((3, heads, band, NA_QBLK), F32),
        compiler_params=_cparams(1),
        name="na_bias_table",
    )(rpb.reshape(-1))


def _na_attn_kernel(q_ref, k0_ref, k1_ref, k2_ref, v0_ref, v1_ref, v2_ref,
                    ck_ref, cv_ref, bias_ref, o_ref):
    dh = NA_HEAD_DIM
    for hh in range(q_ref.shape[1] // dh):
        sl = slice(hh * dh, (hh + 1) * dh)
        q = q_ref[:, sl]
        keys = [k0_ref[:, sl], k1_ref[:, sl], k2_ref[:, sl], ck_ref[0, :, sl].astype(BF16)]
        vals = [v0_ref[:, sl], v1_ref[:, sl], v2_ref[:, sl], cv_ref[0, :, sl].astype(BF16)]
        s = [_dot_nt(kj, q) for kj in keys]
        for jb in range(NA_BAND_BLKS):
            s[jb] = s[jb] + bias_ref[0, hh, jb * NA_QBLK:(jb + 1) * NA_QBLK, :]
        m = s[0].max(axis=0, keepdims=True)
        for sj in s[1:]:
            m = jnp.maximum(m, sj.max(axis=0, keepdims=True))
        p = [jnp.exp(sj - m) for sj in s]
        l = p[0].sum(axis=0, keepdims=True)
        for pj in p[1:]:
            l = l + pj.sum(axis=0, keepdims=True)
        o_t = _dot_tn(vals[0], p[0].astype(BF16))
        for vj, pj in zip(vals[1:], p[1:]):
            o_t = o_t + _dot_tn(vj, pj.astype(BF16))
        o_ref[:, sl] = (o_t * (1.0 / l)).T.astype(BF16)


def _na_attention(qkv, ck, cv, bias, batch, n_tok, heads_per_step=4):
    d = D_MODEL
    width = heads_per_step * NA_HEAD_DIM
    n_hg = NA_HEADS // heads_per_step
    n_rb = n_tok // NA_QBLK
    past = ck.shape[1]

    def qmap(hg, b, rb):
        return (b * n_rb + rb, hg)

    def band_map(col0, off):
        def f(hg, b, rb):
            return (b * n_rb + _band_start_blk(rb, n_rb) + off, col0 + hg)
        return f

    def bias_map(hg, b, rb):
        ty = jnp.where(rb == 0, 0, jnp.where(rb == n_rb - 1, 2, 1))
        return (ty, hg, 0, 0)

    blk = pl.BlockSpec((NA_QBLK, width), qmap)
    in_specs = [blk]
    in_specs += [pl.BlockSpec((NA_QBLK, width), band_map(n_hg, off)) for off in range(NA_BAND_BLKS)]
    in_specs += [pl.BlockSpec((NA_QBLK, width), band_map(2 * n_hg, off)) for off in range(NA_BAND_BLKS)]
    in_specs += [pl.BlockSpec((1, past, width), lambda hg, b, rb: (b, 0, hg))] * 2
    in_specs += [pl.BlockSpec((1, heads_per_step, NA_BAND_BLKS * NA_QBLK, NA_QBLK), bias_map)]
    return pl.pallas_call(
        _na_attn_kernel,
        grid=(n_hg, batch, n_rb),
        in_specs=in_specs,
        out_specs=blk,
        out_shape=jax.ShapeDtypeStruct((batch * n_tok, d), BF16),
        compiler_params=_cparams(3),
        name="na_attention",
    )(qkv, qkv, qkv, qkv, qkv, qkv, qkv, ck, cv, bias)


def _mix_out_kernel(o_ref, w_ref, x_ref, mod_ref, g_ref, x1_ref):
    ga = mod_ref[0, 2:3, :]
    for sl in _row_chunks(o_ref.shape[0]):
        y = _dot(o_ref[sl, :], w_ref[...])
        x1_ref[sl, :] = x_ref[sl, :] + ga * _rms(y, g_ref[1:2, :])


def _mix_out(o, w, x, mod, g, tm=512):
    t, d = x.shape
    groups = mod.shape[0]
    tiles_per_group = t // groups // tm
    return pl.pallas_call(
        _mix_out_kernel,
        grid=(t // tm,),
        in_specs=[
            pl.BlockSpec((tm, d), lambda i: (i, 0)),
            pl.BlockSpec((d, d), lambda i: (0, 0)),
            pl.BlockSpec((tm, d), lambda i: (i, 0)),
            pl.BlockSpec((1, 6, d), lambda i: (i // tiles_per_group, 0, 0)),
            pl.BlockSpec((4, d), lambda i: (0, 0)),
        ],
        out_specs=pl.BlockSpec((tm, d), lambda i: (i, 0)),
        out_shape=jax.ShapeDtypeStruct((t, d), F32),
        compiler_params=_cparams(1),
        name="mix_out",
    )(o, w, x, mod, g)


def _ffn_kernel(x_ref, mod_ref, g_ref, wg_ref, wu_ref, wo_ref, x2_ref, h_scr):
    c = pl.program_id(1)
    last = pl.num_programs(1) - 1
    row_chunks = _row_chunks(x_ref.shape[0])

    def partial_out(h):
        a = (_silu(_dot(h, wg_ref[0])) * _dot(h, wu_ref[0])).astype(BF16)
        return _dot(a, wo_ref[0])

    @pl.when(c == 0)
    def _():
        g, sh, sc = g_ref[2:3, :], mod_ref[0, 3:4, :], mod_ref[0, 4:5, :]
        for sl in row_chunks:
            h = (_rms(x_ref[sl, :], g) * (1.0 + sc) + sh).astype(BF16)
            h_scr[sl, :] = h
            x2_ref[sl, :] = partial_out(h)

    @pl.when((c > 0) & (c < last))
    def _():
        for sl in row_chunks:
            x2_ref[sl, :] += partial_out(h_scr[sl, :])

    @pl.when(c == last)
    def _():
        ga = mod_ref[0, 5:6, :]
        for sl in row_chunks:
            y = x2_ref[sl, :] + partial_out(h_scr[sl, :])
            x2_ref[sl, :] = x_ref[sl, :] + ga * _rms(y, g_ref[3:4, :])


def _ffn(x, mod, g, w_in, w_out, layer, tm=1024, tf=512):
    t, d = x.shape
    hidden = w_out.shape[1]
    n_c = hidden // tf
    groups = mod.shape[0]
    tiles_per_group = t // groups // tm
    row = pl.BlockSpec((tm, d), lambda i, c: (i, 0))
    return pl.pallas_call(
        _ffn_kernel,
        grid=(t // tm, n_c),
        in_specs=[
            row,
            pl.BlockSpec((1, 6, d), lambda i, c: (i // tiles_per_group, 0, 0)),
            pl.BlockSpec((4, d), lambda i, c: (0, 0)),
            pl.BlockSpec((1, d, tf), lambda i, c: (layer, 0, c)),
            pl.BlockSpec((1, d, tf), lambda i, c: (layer, 0, n_c + c)),
            pl.BlockSpec((1, tf, d), lambda i, c: (layer, c, 0)),
        ],
        out_specs=row,
        out_shape=jax.ShapeDtypeStruct((t, d), F32),
        scratch_shapes=[pltpu.VMEM((tm, d), BF16)],
        compiler_params=_cparams(2),
        name="ffn",
    )(x, mod, g, w_in, w_in, w_out)


def _ret_in_kernel(x_ref, mod_ref, g_ref, w_ref, *rest, rope, n_qk, n_q, n_v):
    if rope:
        cos_ref, sin_ref, o_ref, h_scr = rest
    else:
        o_ref, h_scr = rest
    j = pl.program_id(1)
    tn = w_ref.shape[1]
    half = RET_DK // 2
    chunks = _row_chunks(x_ref.shape[0])

    def store_qk(sl, acc, scale):
        for hh in range(tn // RET_DK):
            x = acc[:, hh * RET_DK:(hh + 1) * RET_DK]
            if rope:
                rot = jnp.concatenate(
                    [pltpu.roll(x[:, :half], half // 2, 1), pltpu.roll(x[:, half:], half // 2, 1)], axis=1)
                x = x * cos_ref[sl, :] + rot * sin_ref[sl, :]
            o_ref[sl, hh * RET_DK:(hh + 1) * RET_DK] = (x * scale).astype(BF16)

    @pl.when(j == 0)
    def _():
        g, sc, sh = g_ref[0:1, :], mod_ref[0, 1:2, :], mod_ref[0, 0:1, :]
        for sl in chunks:
            h = (_rms(x_ref[sl, :], g) * (1.0 + sc) + sh).astype(BF16)
            h_scr[sl, :] = h
            store_qk(sl, _dot(h, w_ref[...]), RET_DK ** -0.5)

    @pl.when((j > 0) & (j < n_qk))
    def _():
        scale = jnp.where(j < n_q, RET_DK ** -0.5, 1.0).astype(F32)
        for sl in chunks:
            store_qk(sl, _dot(h_scr[sl, :], w_ref[...]), scale)

    @pl.when((j >= n_qk) & (j < n_qk + n_v))
    def _():
        for sl in chunks:
            o_ref[sl, :] = _dot(h_scr[sl, :], w_ref[...]).astype(BF16)

    @pl.when(j >= n_qk + n_v)
    def _():
        for sl in chunks:
            o_ref[sl, :] = _silu(_dot(h_scr[sl, :], w_ref[...])).astype(BF16)


def _ret_in_proj(x, mod, g, w, rope_tables=None, seq_len=None, tm=1024, tn=1024):
    t, d = x.shape
    n = w.shape[1]
    rope = rope_tables is not None
    groups = mod.shape[0]
    tiles_per_group = t // groups // tm
    in_specs = [
        pl.BlockSpec((tm, d), lambda i, j: (i, 0)),
        pl.BlockSpec((1, 6, d), lambda i, j: (i // tiles_per_group, 0, 0)),
        pl.BlockSpec((4, d), lambda i, j: (0, 0)),
        pl.BlockSpec((d, tn), lambda i, j: (0, j)),
    ]
    args = [x, mod, g, w]
    if rope:
        tiles_per_seq = seq_len // tm
        tab = pl.BlockSpec((tm, RET_DK), lambda i, j: (i % tiles_per_seq, 0))
        in_specs += [tab, tab]
        args += list(rope_tables)
    return pl.pallas_call(
        functools.partial(_ret_in_kernel, rope=rope, n_qk=2 * d // tn, n_q=d // tn, n_v=d // tn),
        grid=(t // tm, n // tn),
        in_specs=in_specs,
        out_specs=pl.BlockSpec((tm, tn), lambda i, j: (i, j)),
        out_shape=jax.ShapeDtypeStruct((t, n), BF16),
        scratch_shapes=[pltpu.VMEM((tm, d), BF16)],
        compiler_params=_cparams(2),
        name="ret_in_rope" if rope else "ret_in",
    )(*args)


def _rope_tables(n_tokens, dim):
    tok = jnp.arange(n_tokens)
    rows = (tok // GRID_W).astype(F32)
    cols = (tok % GRID_W).astype(F32)
    half = dim // 2
    inv = jnp.power(ROPE_BASE, -jnp.arange(0, half, 2, dtype=F32) / half)
    ar = rows[:, None] * inv[None, :]
    ac = cols[:, None] * inv[None, :]
    ang = jnp.concatenate([ar, ar, ac, ac], axis=-1)
    quarter = dim // 4
    sign = jnp.where((jnp.arange(dim) // quarter) % 2 == 0, -1.0, 1.0).astype(F32)
    return jnp.cos(ang), jnp.sin(ang) * sign[None, :]


def _log_sigmoid(x):
    y = -x
    return -(jnp.maximum(y, 0.0) + jnp.log1p(jnp.exp(-jnp.abs(y))))


def _group_norm(o):
    mu = jnp.mean(o, axis=-1, keepdims=True)
    dev = o - mu
    var = jnp.mean(dev * dev, axis=-1, keepdims=True)
    return dev * lax.rsqrt(var + EPS)


def _ret_kernel(dl_ref, q_ref, k_ref, v_ref, gf_ref, gb_ref, *rest,
                n_seq, seq_len, has_state, emit_state, unroll):
    rest = list(rest)
    if has_state:
        s0f_ref, s0b_ref = rest[:2]
        rest = rest[2:]
    y_ref = rest.pop(0)
    if emit_state:
        sf_out_ref, sb_out_ref = rest[:2]
        rest = rest[2:]
    of_scr, pb_scr, st_scr = rest

    head = pl.program_id(1)
    C = RET_CHUNK
    dk = RET_DK
    n_chunks = seq_len // C

    def lg(shape, direction):
        return _log_sigmoid(jnp.full(shape, dl_ref[direction, head], F32))

    ii = lax.broadcasted_iota(jnp.int32, (C, C), 0)
    jj = lax.broadcasted_iota(jnp.int32, (C, C), 1)
    causal = ii >= jj
    anti = jj >= ii
    dist = jnp.abs(ii - jj).astype(F32)
    decay_f = jnp.where(causal, jnp.exp(jnp.where(causal, dist, 0.0) * lg((C, C), 0)), 0.0)
    decay_b = jnp.where(anti, jnp.exp(jnp.where(anti, dist, 0.0) * lg((C, C), 1)), 0.0)
    pos = lax.broadcasted_iota(jnp.int32, (C, dk), 0).astype(F32)
    lgf = lg((C, dk), 0)
    lgb = lg((C, dk), 1)
    qd_f = jnp.exp((pos + 1.0) * lgf)
    kd_f = jnp.exp((C - 1.0 - pos) * lgf)
    qd_b = jnp.exp((C - pos) * lgb)
    kd_b = jnp.exp(pos * lgb)
    cd_f = jnp.exp(C * lg((1, dk), 0))
    cd_b = jnp.exp(C * lg((1, dk), 1))

    def chunk_rows(s, t):
        return pl.ds(pl.multiple_of(s * seq_len + t * C, C), C)

    def advance(s, q, k, v, scores, qd, kd, cd):
        state = st_scr[s]
        o = _dot(scores, v) + _dot(q, state.astype(BF16)) * qd
        st_scr[s] = state * cd + _dot_tn((k.astype(F32) * kd).astype(BF16), v)
        return o

    def fwd_chunk(s, t):
        rows = chunk_rows(s, t)
        q, k, v = q_ref[rows, :], k_ref[rows, :], v_ref[rows, :]
        raw = _dot_nt(q, k)
        pb_scr[rows, :] = (raw * decay_b).astype(BF16)
        o = advance(s, q, k, v, (raw * decay_f).astype(BF16), qd_f, kd_f, cd_f)
        of_scr[rows, :] = _group_norm(o)

    def bwd_chunk(s, t):
        rows = chunk_rows(s, t)
        q, k, v = q_ref[rows, :], k_ref[rows, :], v_ref[rows, :]
        o = advance(s, q, k, v, pb_scr[rows, :], qd_b, kd_b, cd_b)
        y = (gf_ref[rows, :].astype(F32) * of_scr[rows, :]
             + gb_ref[rows, :].astype(F32) * _group_norm(o))
        y_ref[rows, :] = y.astype(BF16)

    def sweep(chunk_fn, reverse):
        def at(t):
            for s in range(n_seq):
                chunk_fn(s, n_chunks - 1 - t if reverse else t)

        if n_chunks <= 4:
            for t in range(n_chunks):
                at(t)
        else:
            lax.fori_loop(0, n_chunks, lambda t, carry: (at(t), carry)[1], 0, unroll=unroll)

    for s in range(n_seq):
        st_scr[s] = s0f_ref[s, 0] if has_state else jnp.zeros((dk, dk), F32)
    sweep(fwd_chunk, reverse=False)
    for s in range(n_seq):
        if emit_state:
            sf_out_ref[s, 0, 0] = st_scr[s]
        st_scr[s] = s0b_ref[s, 0] if has_state else jnp.zeros((dk, dk), F32)
    sweep(bwd_chunk, reverse=True)
    if emit_state:
        for s in range(n_seq):
            sb_out_ref[s, 0, 0] = st_scr[s]


def _retention(proj, decay_logit, seq_len, n_seq, state_f=None, state_b=None, emit_state=False):
    t = proj.shape[0]
    d = D_MODEL
    dk = RET_DK
    heads = RET_HEADS
    rows = n_seq * seq_len
    has_state = state_f is not None

    def col(offset):
        return pl.BlockSpec((rows, dk), lambda b, h: (b, offset * heads + h))

    in_specs = [pl.BlockSpec(memory_space=pltpu.SMEM)] + [col(o) for o in range(5)]
    args = [decay_logit, proj, proj, proj, proj, proj]
    if has_state:
        st = pl.BlockSpec((n_seq, 1, dk, dk), lambda b, h: (b, h, 0, 0))
        in_specs += [st, st]
        args += [state_f, state_b]
    out_specs = [pl.BlockSpec((rows, dk), lambda b, h: (b, h))]
    out_shape = [jax.ShapeDtypeStruct((t, d), BF16)]
    if emit_state:
        n_batch = t // seq_len
        st_out = pl.BlockSpec((n_seq, 1, 1, dk, dk), lambda b, h: (b, 0, h, 0, 0))
        out_specs += [st_out, st_out]
        out_shape += [jax.ShapeDtypeStruct((n_batch, 1, heads, dk, dk), F32)] * 2
    return pl.pallas_call(
        functools.partial(_ret_kernel, n_seq=n_seq, seq_len=seq_len,
                          has_state=has_state, emit_state=emit_state, unroll=4),
        grid=(t // rows, heads),
        in_specs=in_specs,
        out_specs=out_specs,
        out_shape=out_shape,
        scratch_shapes=[pltpu.VMEM((rows, dk), F32), pltpu.VMEM((rows, RET_CHUNK), BF16),
                        pltpu.VMEM((n_seq, dk, dk), F32)],
        compiler_params=_cparams(2),
        name="retention_state" if emit_state else "retention",
    )(*args)


def kernel(x_prompt, x_sample, cache_na_k, cache_na_v, state_ret_fwd, state_ret_bwd, c, c_ctx,
           w_ada, b_ada, g_norm, na_w_qkv, na_w_o, na_rpb, ret_w_in, ret_w_o, ret_decay_logit,
           ffn_w_in, ffn_w_out):
    bp, seq, d = x_prompt.shape
    bs, n_tok, _ = x_sample.shape
    past = cache_na_k.shape[2]
    xp = x_prompt.reshape(bp * seq, d)
    xs = x_sample.reshape(bs * n_tok, d)

    cond = jnp.concatenate([c_ctx[None, :], c, jnp.zeros((8 - 1 - bs, d), F32)], axis=0)
    mod = _ada_modulation(cond, w_ada, b_ada).reshape(w_ada.shape[0], 8, 6, d)
    mod_p = [mod[l, 0:1] for l in range(2)]
    mod_s = [mod[l, 1:1 + bs] for l in range(2)]

    w_qkv = na_w_qkv[0].astype(BF16)
    w_na_o = na_w_o[0].astype(BF16)
    w_ret_in = ret_w_in[0].astype(BF16)
    w_ret_o = ret_w_o[0].astype(BF16)
    w_ffn_in = ffn_w_in.astype(BF16)
    w_ffn_out = ffn_w_out.astype(BF16)

    qkv_p, k_p, v_p = _qkv_proj(xp, mod_p[0], g_norm[0], w_qkv, emit_kv=True)
    (qkv_s,) = _qkv_proj(xs, mod_s[0], g_norm[0], w_qkv, emit_kv=False)
    o_p = _ctx_attention(qkv_p, seq)
    bias = _na_bias_table(na_rpb[0], n_tok // GRID_W)
    ck = cache_na_k[:, 0].reshape(bs, past, d)
    cv = cache_na_v[:, 0].reshape(bs, past, d)
    o_s = _na_attention(qkv_s, ck, cv, bias, bs, n_tok)

    xp = _mix_out(o_p, w_na_o, xp, mod_p[0], g_norm[0])
    xs = _mix_out(o_s, w_na_o, xs, mod_s[0], g_norm[0])
    xp = _ffn(xp, mod_p[0], g_norm[0], w_ffn_in, w_ffn_out, 0)
    xs = _ffn(xs, mod_s[0], g_norm[0], w_ffn_in, w_ffn_out, 0)

    proj_p = _ret_in_proj(xp, mod_p[1], g_norm[1], w_ret_in)
    proj_s = _ret_in_proj(xs, mod_s[1], g_norm[1], w_ret_in, _rope_tables(n_tok, RET_DK), n_tok)
    y_p, sf, sb = _retention(proj_p, ret_decay_logit[0], seq, 8, emit_state=True)
    (y_s,) = _retention(proj_s, ret_decay_logit[0], n_tok, 1,
                        state_f=state_ret_fwd[:, 0], state_b=state_ret_bwd[:, 0])

    xp = _mix_out(y_p, w_ret_o, xp, mod_p[1], g_norm[1])
    xs = _mix_out(y_s, w_ret_o, xs, mod_s[1], g_norm[1])
    xp = _ffn(xp, mod_p[1], g_norm[1], w_ffn_in, w_ffn_out, 1)
    xs = _ffn(xs, mod_s[1], g_norm[1], w_ffn_in, w_ffn_out, 1)

    kv_shape = (bp, 1, seq, NA_HEADS, NA_HEAD_DIM)
    return (xp.reshape(bp, seq, d), xs.reshape(bs, n_tok, d),
            k_p.reshape(kv_shape), v_p.reshape(kv_shape),
            sf.astype(x_prompt.dtype), sb.astype(x_prompt.dtype))
```

```python
import functools

import jax
import jax.numpy as jnp
from jax import lax
from jax.experimental import pallas as pl
from jax.experimental.pallas import tpu as pltpu

D_MODEL = 2048
GRID_W = 64
NA_HEADS = 16
NA_HEAD_DIM = D_MODEL // NA_HEADS
NA_KH = 8
NA_KW = 16
RET_HEADS = 8
RET_DK = D_MODEL // RET_HEADS
RET_CHUNK = 128
FFN_HIDDEN = -(-8 * D_MODEL // (3 * 256)) * 256
ROPE_BASE = 10000.0
EPS = 1e-6

F32 = jnp.float32
BF16 = jnp.bfloat16
MASK_VALUE = -1e30

VMEM_LIMIT_BYTES = 56 * 1024 * 1024
ROW_CHUNK = 256
NA_QROWS = 4
NA_QBLK = NA_QROWS * GRID_W
NA_BAND_BLKS = 3


def _cparams(n_axes, **extra):
    return pltpu.CompilerParams(
        dimension_semantics=("arbitrary",) * n_axes,
        vmem_limit_bytes=VMEM_LIMIT_BYTES, **extra)


def _silu(x):
    return x / (1.0 + jnp.exp(-x))


def _rms(x, g):
    ms = jnp.mean(x * x, axis=-1, keepdims=True)
    return x * lax.rsqrt(ms + EPS) * g


def _dot(a, b):
    return jnp.dot(a, b, preferred_element_type=F32)


def _dot_nt(a, b):
    return lax.dot_general(a, b, (((1,), (1,)), ((), ())), preferred_element_type=F32)


def _dot_tn(a, b):
    return lax.dot_general(a, b, (((0,), (0,)), ((), ())), preferred_element_type=F32)


def _ada_kernel(cond_ref, w_ref, b_ref, o_ref):
    a = _silu(cond_ref[...]).astype(BF16)
    o_ref[0] = _dot(a, w_ref[0].astype(BF16)) + b_ref[0]


def _ada_modulation(cond, w_ada, b_ada, tn=1024):
    depth, d, n = w_ada.shape
    rows = cond.shape[0]
    return pl.pallas_call(
        _ada_kernel,
        grid=(depth, n // tn),
        in_specs=[
            pl.BlockSpec((rows, d), lambda l, j: (0, 0)),
            pl.BlockSpec((1, d, tn), lambda l, j: (l, 0, j)),
            pl.BlockSpec((1, 1, tn), lambda l, j: (l, 0, j)),
        ],
        out_specs=pl.BlockSpec((1, rows, tn), lambda l, j: (l, 0, j)),
        out_shape=jax.ShapeDtypeStruct((depth, rows, n), F32),
        compiler_params=_cparams(2),
        name="ada_modulation",
    )(cond, w_ada, b_ada.reshape(depth, 1, n))


def _cast_kernel(w_ref, o_ref):
    o_ref[...] = w_ref[...].astype(o_ref.dtype)


def _cast_bf16(w, rows_per_block=512):
    depth, rows, cols = w.shape
    spec = pl.BlockSpec((1, rows_per_block, cols), lambda l, r: (l, r, 0))
    return pl.pallas_call(
        _cast_kernel,
        grid=(depth, rows // rows_per_block),
        in_specs=[spec],
        out_specs=spec,
        out_shape=jax.ShapeDtypeStruct(w.shape, BF16),
        compiler_params=_cparams(2),
        name="cast_bf16",
    )(w)


def _row_chunks(n_rows):
    return [slice(r * ROW_CHUNK, (r + 1) * ROW_CHUNK) for r in range(n_rows // ROW_CHUNK)]


def _qkv_kernel(x_ref, mod_ref, g_ref, w_ref, *rest, nq, nk, emit_kv):
    if emit_kv:
        o_ref, k_ref, v_ref, h_scr = rest
    else:
        o_ref, h_scr = rest
    j = pl.program_id(1)
    chunks = _row_chunks(x_ref.shape[0])
    q_scale = NA_HEAD_DIM ** -0.5

    @pl.when(j == 0)
    def _():
        g, sc, sh = g_ref[0:1, :], mod_ref[0, 1:2, :], mod_ref[0, 0:1, :]
        for sl in chunks:
            h = (_rms(x_ref[sl, :], g) * (1.0 + sc) + sh).astype(BF16)
            h_scr[sl, :] = h
            o_ref[sl, :] = (_dot(h, w_ref[...]) * q_scale).astype(BF16)

    @pl.when((j > 0) & (j < nq))
    def _():
        for sl in chunks:
            o_ref[sl, :] = (_dot(h_scr[sl, :], w_ref[...]) * q_scale).astype(BF16)

    @pl.when((j >= nq) & (j < nq + nk))
    def _():
        for sl in chunks:
            acc = _dot(h_scr[sl, :], w_ref[...])
            o_ref[sl, :] = acc.astype(BF16)
            if emit_kv:
                k_ref[sl, :] = acc

    @pl.when(j >= nq + nk)
    def _():
        for sl in chunks:
            acc = _dot(h_scr[sl, :], w_ref[...])
            o_ref[sl, :] = acc.astype(BF16)
            if emit_kv:
                v_ref[sl, :] = acc


def _qkv_proj(x, mod, g, w, *, emit_kv, tn=1024):
    t, d = x.shape
    tm = 1024
    n = w.shape[1]
    groups = mod.shape[0]
    tiles_per_group = t // groups // tm
    nq = d // tn
    nk = d // tn
    out_shape = [jax.ShapeDtypeStruct((t, n), BF16)]
    out_specs = [pl.BlockSpec((tm, tn), lambda i, j: (i, j))]
    if emit_kv:
        out_shape += [jax.ShapeDtypeStruct((t, d), F32)] * 2
        out_specs += [
            pl.BlockSpec((tm, tn), lambda i, j: (i, jnp.clip(j - nq, 0, nk - 1))),
            pl.BlockSpec((tm, tn), lambda i, j: (i, jnp.clip(j - nq - nk, 0, nk - 1))),
        ]
    return pl.pallas_call(
        functools.partial(_qkv_kernel, nq=nq, nk=nk, emit_kv=emit_kv),
        grid=(t // tm, n // tn),
        in_specs=[
            pl.BlockSpec((tm, d), lambda i, j: (i, 0)),
            pl.BlockSpec((1, 6, d), lambda i, j: (i // tiles_per_group, 0, 0)),
            pl.BlockSpec((4, d), lambda i, j: (0, 0)),
            pl.BlockSpec((d, tn), lambda i, j: (0, j)),
        ],
        out_specs=out_specs,
        out_shape=out_shape,
        scratch_shapes=[pltpu.VMEM((tm, d), BF16)],
        compiler_params=_cparams(2),
        name="qkv_proj_kv" if emit_kv else "qkv_proj",
    )(x, mod, g, w)


def _pipelined_heads(n_heads, n_blk, score_blk, value_blk, store):
    s_new = p_new = l_new = None
    for step in range(n_heads + 2):
        h_s, h_e, h_o = step, step - 1, step - 2
        s_cur, s_new = s_new, []
        p_cur, p_new = p_new, []
        l_cur, l_new = l_new, None
        exp_on = 0 <= h_e < n_heads
        out_on = 0 <= h_o < n_heads
        if exp_on:
            m = s_cur[0].max(axis=0, keepdims=True)
            for sj in s_cur[1:]:
                m = jnp.maximum(m, sj.max(axis=0, keepdims=True))
        o_t = None
        for jb in range(n_blk):
            if h_s < n_heads:
                s_new.append(score_blk(h_s, jb))
            if exp_on:
                p = jnp.exp(s_cur[jb] - m)
                part = p.sum(axis=0, keepdims=True)
                l_new = part if l_new is None else l_new + part
                p_new.append(p.astype(BF16))
            if out_on:
                part = _dot_tn(value_blk(h_o, jb), p_cur[jb])
                o_t = part if o_t is None else o_t + part
        if out_on:
            store(h_o, (o_t * (1.0 / l_cur)).T)


def _ctx_attn_kernel(q_ref, k_ref, v_ref, o_ref):
    for h in range(NA_HEADS):
        sl = slice(h * NA_HEAD_DIM, (h + 1) * NA_HEAD_DIM)
        s = _dot_nt(q_ref[:, sl], k_ref[:, sl])
        m = jnp.max(s, axis=-1, keepdims=True)
        p = jnp.exp(s - m)
        l = jnp.sum(p, axis=-1, keepdims=True)
        o = _dot(p.astype(BF16), v_ref[:, sl])
        o_ref[:, sl] = (o * (1.0 / l)).astype(BF16)


def _ctx_attention(qkv, seq):
    t = qkv.shape[0]
    d = D_MODEL
    return pl.pallas_call(
        _ctx_attn_kernel,
        grid=(t // seq,),
        in_specs=[
            pl.BlockSpec((seq, d), lambda b: (b, 0)),
            pl.BlockSpec((seq, d), lambda b: (b, 1)),
            pl.BlockSpec((seq, d), lambda b: (b, 2)),
        ],
        out_specs=pl.BlockSpec((seq, d), lambda b: (b, 0)),
        out_shape=jax.ShapeDtypeStruct((t, d), BF16),
        compiler_params=_cparams(1),
        name="ctx_attention",
    )(qkv, qkv, qkv)


def _band_start_blk(rb, n_rb):
    return jnp.clip(rb - 1, 0, n_rb - NA_BAND_BLKS)


def _bias_tile_plan(n_grid_rows):
    n_rb = n_grid_rows // NA_QROWS
    plans = []
    for rb in (0, 1, n_rb - 1):
        r0 = rb * NA_QROWS
        a = min(max(rb - 1, 0), n_rb - NA_BAND_BLKS) * NA_QROWS
        rows = []
        for j in range(NA_BAND_BLKS * NA_QROWS):
            kr = a + j
            pairs = []
            for ip in range(NA_QROWS // 2):
                drs = []
                for i in (2 * ip, 2 * ip + 1):
                    r = r0 + i
                    rs = min(max(r - NA_KH // 2, 0), n_grid_rows - NA_KH)
                    drs.append(kr - r + NA_KH - 1 if rs <= kr < rs + NA_KH else None)
                pairs.append(tuple(drs))
            rows.append(pairs)
        plans.append(rows)
    return plans


def _na_bias_kernel(rpb_ref, o_ref, *, plan):
    h = pl.program_id(0)
    n_dr = 2 * NA_KH - 1
    n_dc = 2 * NA_KW - 1
    shape = (GRID_W, 2 * GRID_W)
    kc = lax.broadcasted_iota(jnp.int32, shape, 0)
    lane = lax.broadcasted_iota(jnp.int32, shape, 1)
    qc = lane & (GRID_W - 1)
    right = lane >= GRID_W
    diff = kc - qc
    cstart = jnp.clip(qc - NA_KW // 2, 0, GRID_W - NA_KW)
    in_window = (kc >= cstart) & (kc < cstart + NA_KW)
    masked = jnp.full(shape, MASK_VALUE, F32)

    used = sorted({dr for rows in plan for pairs in rows for pr in pairs for dr in pr if dr is not None})
    tiles = {}
    for dr in used:
        t = masked
        for dc in range(n_dc):
            val = rpb_ref[h * (n_dr * n_dc) + dr * n_dc + dc]
            t = jnp.where(diff == dc - (NA_KW - 1), val, t)
        tiles[dr] = jnp.where(in_window, t, MASK_VALUE)
    tiles[None] = masked

    for ty, rows in enumerate(plan):
        for j, pairs in enumerate(rows):
            for ip, (dl, dr) in enumerate(pairs):
                tile = tiles[dl] if dl == dr else jnp.where(right, tiles[dr], tiles[dl])
                o_ref[ty, 0, j * GRID_W:(j + 1) * GRID_W, ip * 2 * GRID_W:(ip + 1) * 2 * GRID_W] = tile


def _na_bias_table(rpb, n_grid_rows):
    heads = rpb.shape[0]
    band = NA_BAND_BLKS * NA_QBLK
    plan = _bias_tile_plan(n_grid_rows)
    return pl.pallas_call(
        functools.partial(_na_bias_kernel, plan=plan),
        grid=(heads,),
        in_specs=[pl.BlockSpec(memory_space=pltpu.SMEM)],
        out_specs=pl.BlockSpec((3, 1, band, NA_QBLK), lambda h: (0, h, 0, 0)),
        out_shape=jax.ShapeDtypeStruct((3, heads, band, NA_QBLK), F32),
        compiler_params=_cparams(1),
        name="na_bias_table",
    )(rpb.reshape(-1))


def _na_attn_kernel(q_ref, k0_ref, k1_ref, k2_ref, v0_ref, v1_ref, v2_ref,
                    ck_ref, cv_ref, bias_ref, o_ref):
    dh = NA_HEAD_DIM
    n_heads = q_ref.shape[1] // dh

    n_blk = NA_BAND_BLKS + 1
    k_refs = (k0_ref, k1_ref, k2_ref)
    v_refs = (v0_ref, v1_ref, v2_ref)

    def head_cols(hh):
        return slice(hh * dh, (hh + 1) * dh)

    def score_blk(hh, jb):
        sl = head_cols(hh)
        if jb < NA_BAND_BLKS:
            return (_dot_nt(k_refs[jb][:, sl], q_ref[:, sl])
                    + bias_ref[0, hh, jb * NA_QBLK:(jb + 1) * NA_QBLK, :])
        return _dot_nt(ck_ref[0, :, sl].astype(BF16), q_ref[:, sl])

    def value_blk(hh, jb):
        sl = head_cols(hh)
        return v_refs[jb][:, sl] if jb < NA_BAND_BLKS else cv_ref[0, :, sl].astype(BF16)

    def store(hh, o):
        o_ref[:, head_cols(hh)] = o.astype(BF16)

    _pipelined_heads(n_heads, n_blk, score_blk, value_blk, store)


def _na_attention(qkv, ck, cv, bias, batch, n_tok, heads_per_step=8):
    d = D_MODEL
    width = heads_per_step * NA_HEAD_DIM
    n_hg = NA_HEADS // heads_per_step
    n_rb = n_tok // NA_QBLK
    past = ck.shape[1]

    def qmap(hg, b, rb):
        return (b * n_rb + rb, hg)

    def band_map(col0, off):
        def f(hg, b, rb):
            return (b * n_rb + _band_start_blk(rb, n_rb) + off, col0 + hg)
        return f

    def bias_map(hg, b, rb):
        ty = jnp.where(rb == 0, 0, jnp.where(rb == n_rb - 1, 2, 1))
        return (ty, hg, 0, 0)

    blk = pl.BlockSpec((NA_QBLK, width), qmap)
    in_specs = [blk]
    in_specs += [pl.BlockSpec((NA_QBLK, width), band_map(n_hg, off)) for off in range(NA_BAND_BLKS)]
    in_specs += [pl.BlockSpec((NA_QBLK, width), band_map(2 * n_hg, off)) for off in range(NA_BAND_BLKS)]
    in_specs += [pl.BlockSpec((1, past, width), lambda hg, b, rb: (b, 0, hg))] * 2
    in_specs += [pl.BlockSpec((1, heads_per_step, NA_BAND_BLKS * NA_QBLK, NA_QBLK), bias_map)]
    return pl.pallas_call(
        _na_attn_kernel,
        grid=(n_hg, batch, n_rb),
        in_specs=in_specs,
        out_specs=blk,
        out_shape=jax.ShapeDtypeStruct((batch * n_tok, d), BF16),
        compiler_params=_cparams(3),
        name="na_attention",
    )(qkv, qkv, qkv, qkv, qkv, qkv, qkv, ck, cv, bias)


def _mix_out_kernel(o_ref, w_ref, x_ref, mod_ref, g_ref, x1_ref):
    ga = mod_ref[0, 2:3, :]
    chunks = _row_chunks(o_ref.shape[0])
    y_next = _dot(o_ref[chunks[0], :], w_ref[...])
    for idx, sl in enumerate(chunks):
        y = y_next
        if idx + 1 < len(chunks):
            y_next = _dot(o_ref[chunks[idx + 1], :], w_ref[...])
        x1_ref[sl, :] = x_ref[sl, :] + ga * _rms(y, g_ref[1:2, :])


def _mix_out(o, w, x, mod, g, tm=512):
    t, d = x.shape
    groups = mod.shape[0]
    tiles_per_group = t // groups // tm
    return pl.pallas_call(
        _mix_out_kernel,
        grid=(t // tm,),
        in_specs=[
            pl.BlockSpec((tm, d), lambda i: (i, 0)),
            pl.BlockSpec((d, d), lambda i: (0, 0)),
            pl.BlockSpec((tm, d), lambda i: (i, 0)),
            pl.BlockSpec((1, 6, d), lambda i: (i // tiles_per_group, 0, 0)),
            pl.BlockSpec((4, d), lambda i: (0, 0)),
        ],
        out_specs=pl.BlockSpec((tm, d), lambda i: (i, 0)),
        out_shape=jax.ShapeDtypeStruct((t, d), F32),
        compiler_params=_cparams(1),
        name="mix_out",
    )(o, w, x, mod, g)


def _ffn_kernel(x_ref, mod_ref, g_ref, wg_ref, wu_ref, wo_ref, x2_ref, h_scr):
    c = pl.program_id(1)
    last = pl.num_programs(1) - 1
    row_chunks = _row_chunks(x_ref.shape[0])

    def hidden(h):
        return (_silu(_dot(h, wg_ref[0])) * _dot(h, wu_ref[0])).astype(BF16)

    def over_chunks(get_h, emit):
        a_next = hidden(get_h(row_chunks[0]))
        for idx, sl in enumerate(row_chunks):
            a = a_next
            if idx + 1 < len(row_chunks):
                a_next = hidden(get_h(row_chunks[idx + 1]))
            emit(sl, _dot(a, wo_ref[0]))

    @pl.when(c == 0)
    def _():
        g, sh, sc = g_ref[2:3, :], mod_ref[0, 3:4, :], mod_ref[0, 4:5, :]

        def normed(sl):
            h = (_rms(x_ref[sl, :], g) * (1.0 + sc) + sh).astype(BF16)
            h_scr[sl, :] = h
            return h

        def assign(sl, part):
            x2_ref[sl, :] = part

        over_chunks(normed, assign)

    @pl.when((c > 0) & (c < last))
    def _():
        def accumulate(sl, part):
            x2_ref[sl, :] += part

        over_chunks(lambda sl: h_scr[sl, :], accumulate)

    @pl.when(c == last)
    def _():
        ga = mod_ref[0, 5:6, :]

        def finish(sl, part):
            y = x2_ref[sl, :] + part
            x2_ref[sl, :] = x_ref[sl, :] + ga * _rms(y, g_ref[3:4, :])

        over_chunks(lambda sl: h_scr[sl, :], finish)


def _ffn(x, mod, g, w_in, w_out, layer, tm=1024, tf=512):
    t, d = x.shape
    hidden = w_out.shape[1]
    n_c = hidden // tf
    groups = mod.shape[0]
    tiles_per_group = t // groups // tm
    row = pl.BlockSpec((tm, d), lambda i, c: (i, 0))
    return pl.pallas_call(
        _ffn_kernel,
        grid=(t // tm, n_c),
        in_specs=[
            row,
            pl.BlockSpec((1, 6, d), lambda i, c: (i // tiles_per_group, 0, 0)),
            pl.BlockSpec((4, d), lambda i, c: (0, 0)),
            pl.BlockSpec((1, d, tf), lambda i, c: (layer, 0, c)),
            pl.BlockSpec((1, d, tf), lambda i, c: (layer, 0, n_c + c)),
            pl.BlockSpec((1, tf, d), lambda i, c: (layer, c, 0)),
        ],
        out_specs=row,
        out_shape=jax.ShapeDtypeStruct((t, d), F32),
        scratch_shapes=[pltpu.VMEM((tm, d), BF16)],
        compiler_params=_cparams(2),
        name="ffn",
    )(x, mod, g, w_in, w_in, w_out)


def _ret_in_kernel(x_ref, mod_ref, g_ref, w_ref, *rest, rope, n_qk, n_q, n_v):
    if rope:
        cos_ref, sin_ref, o_ref, h_scr = rest
    else:
        o_ref, h_scr = rest
    j = pl.program_id(1)
    tn = w_ref.shape[1]
    half = RET_DK // 2
    chunks = _row_chunks(x_ref.shape[0])

    def store_qk(sl, acc, scale):
        for hh in range(tn // RET_DK):
            cols = slice(hh * RET_DK, (hh + 1) * RET_DK)
            x = acc[:, cols]
            if rope:
                rot = jnp.concatenate(
                    [pltpu.roll(x[:, :half], half // 2, 1), pltpu.roll(x[:, half:], half // 2, 1)], axis=1)
                x = x * cos_ref[sl, :] + rot * sin_ref[sl, :]
            o_ref[sl, cols] = (x * scale).astype(BF16)

    @pl.when(j == 0)
    def _():
        g, sc, sh = g_ref[0:1, :], mod_ref[0, 1:2, :], mod_ref[0, 0:1, :]
        for sl in chunks:
            h = (_rms(x_ref[sl, :], g) * (1.0 + sc) + sh).astype(BF16)
            h_scr[sl, :] = h
            store_qk(sl, _dot(h, w_ref[...]), RET_DK ** -0.5)

    @pl.when((j > 0) & (j < n_qk))
    def _():
        scale = jnp.where(j < n_q, RET_DK ** -0.5, 1.0).astype(F32)
        for sl in chunks:
            store_qk(sl, _dot(h_scr[sl, :], w_ref[...]), scale)

    @pl.when((j >= n_qk) & (j < n_qk + n_v))
    def _():
        for sl in chunks:
            o_ref[sl, :] = _dot(h_scr[sl, :], w_ref[...]).astype(BF16)

    @pl.when(j >= n_qk + n_v)
    def _():
        for sl in chunks:
            o_ref[sl, :] = _silu(_dot(h_scr[sl, :], w_ref[...])).astype(BF16)


def _ret_in_proj(x, mod, g, w, rope_tables=None, seq_len=None, tm=1024, tn=1024):
    t, d = x.shape
    n = w.shape[1]
    rope = rope_tables is not None
    groups = mod.shape[0]
    tiles_per_group = t // groups // tm
    in_specs = [
        pl.BlockSpec((tm, d), lambda i, j: (i, 0)),
        pl.BlockSpec((1, 6, d), lambda i, j: (i // tiles_per_group, 0, 0)),
        pl.BlockSpec((4, d), lambda i, j: (0, 0)),
        pl.BlockSpec((d, tn), lambda i, j: (0, j)),
    ]
    args = [x, mod, g, w]
    if rope:
        tiles_per_seq = seq_len // tm
        tab = pl.BlockSpec((tm, RET_DK), lambda i, j: (i % tiles_per_seq, 0))
        in_specs += [tab, tab]
        args += list(rope_tables)
    return pl.pallas_call(
        functools.partial(_ret_in_kernel, rope=rope, n_qk=2 * d // tn, n_q=d // tn, n_v=d // tn),
        grid=(t // tm, n // tn),
        in_specs=in_specs,
        out_specs=pl.BlockSpec((tm, tn), lambda i, j: (i, j)),
        out_shape=jax.ShapeDtypeStruct((t, n), BF16),
        scratch_shapes=[pltpu.VMEM((tm, d), BF16)],
        compiler_params=_cparams(2),
        name="ret_in_rope" if rope else "ret_in",
    )(*args)


def _rope_tables(n_tokens, dim):
    n_rows = n_tokens // GRID_W
    half = dim // 2
    quarter = dim // 4
    inv = jnp.power(ROPE_BASE, -jnp.arange(0, half, 2, dtype=F32) / half)
    ang_r = jnp.arange(n_rows, dtype=F32)[:, None] * inv[None, :]
    ang_c = jnp.arange(GRID_W, dtype=F32)[:, None] * inv[None, :]

    def table(fn, lo_sign):
        by_row = jnp.broadcast_to(fn(ang_r)[:, None, :], (n_rows, GRID_W, quarter))
        by_col = jnp.broadcast_to(fn(ang_c)[None, :, :], (n_rows, GRID_W, quarter))
        full = jnp.concatenate([lo_sign * by_row, by_row, lo_sign * by_col, by_col], axis=-1)
        return full.reshape(n_tokens, dim)

    return table(jnp.cos, 1.0), table(jnp.sin, -1.0)


def _log_sigmoid(x):
    y = -x
    return -(jnp.maximum(y, 0.0) + jnp.log1p(jnp.exp(-jnp.abs(y))))


def _group_norm(o):
    mu = jnp.mean(o, axis=-1, keepdims=True)
    dev = o - mu
    var = jnp.mean(dev * dev, axis=-1, keepdims=True)
    return dev * lax.rsqrt(var + EPS)


def _ret_kernel(dl_ref, q_ref, k_ref, v_ref, gf_ref, gb_ref, *rest,
                n_seq, seq_len, has_state, emit_state, unroll):
    rest = list(rest)
    if has_state:
        s0f_ref, s0b_ref = rest[:2]
        rest = rest[2:]
    y_ref = rest.pop(0)
    if emit_state:
        sf_out_ref, sb_out_ref = rest[:2]
        rest = rest[2:]
    of_scr, pb_scr, st_scr = rest

    head = pl.program_id(1)
    C = RET_CHUNK
    dk = RET_DK
    n_chunks = seq_len // C

    def lg(shape, direction):
        return _log_sigmoid(jnp.full(shape, dl_ref[direction, head], F32))

    ii = lax.broadcasted_iota(jnp.int32, (C, C), 0)
    jj = lax.broadcasted_iota(jnp.int32, (C, C), 1)
    causal = ii >= jj
    anti = jj >= ii
    dist = jnp.abs(ii - jj).astype(F32)
    decay_f = jnp.where(causal, jnp.exp(jnp.where(causal, dist, 0.0) * lg((C, C), 0)), 0.0)
    decay_b = jnp.where(anti, jnp.exp(jnp.where(anti, dist, 0.0) * lg((C, C), 1)), 0.0)
    pos = lax.broadcasted_iota(jnp.int32, (C, dk), 0).astype(F32)
    lgf = lg((C, dk), 0)
    lgb = lg((C, dk), 1)
    qd_f = jnp.exp((pos + 1.0) * lgf)
    kd_f = jnp.exp((C - 1.0 - pos) * lgf)
    qd_b = jnp.exp((C - pos) * lgb)
    kd_b = jnp.exp(pos * lgb)
    cd_f = jnp.exp(C * lg((1, dk), 0))
    cd_b = jnp.exp(C * lg((1, dk), 1))

    def chunk_rows(s, t):
        return pl.ds(pl.multiple_of(s * seq_len + t * C, C), C)

    def key_value(k, v, kd):
        return _dot_tn((k.astype(F32) * kd).astype(BF16), v)

    def fwd_local(s, t):
        rows = chunk_rows(s, t)
        q, k, v = q_ref[rows, :], k_ref[rows, :], v_ref[rows, :]
        raw = _dot_nt(q, k)
        pb_scr[rows, :] = (raw * decay_b).astype(BF16)
        return rows, q, _dot((raw * decay_f).astype(BF16), v), key_value(k, v, kd_f)

    def fwd_finish(local, state):
        rows, q, inner, kv = local
        of_scr[rows, :] = _group_norm(inner + _dot(q, state.astype(BF16)) * qd_f)
        return state * cd_f + kv

    def bwd_local(s, t):
        rows = chunk_rows(s, t)
        q, k, v = q_ref[rows, :], k_ref[rows, :], v_ref[rows, :]
        return rows, q, _dot(pb_scr[rows, :], v), key_value(k, v, kd_b)

    def bwd_finish(local, state):
        rows, q, inner, kv = local
        o = inner + _dot(q, state.astype(BF16)) * qd_b
        y = (gf_ref[rows, :].astype(F32) * of_scr[rows, :]
             + gb_ref[rows, :].astype(F32) * _group_norm(o))
        y_ref[rows, :] = y.astype(BF16)
        return state * cd_b + kv

    def sweep(local_fn, finish_fn, reverse):
        group = min(unroll, n_chunks)

        def run_group(gi):
            items = [(s, gi * group + c) for s in range(n_seq) for c in range(group)]
            states = {}

            def local_at(item):
                s, t = item
                return local_fn(s, n_chunks - 1 - t if reverse else t)

            nxt = local_at(items[0])
            for idx, (s, t) in enumerate(items):
                cur = nxt
                if idx + 1 < len(items):
                    nxt = local_at(items[idx + 1])
                state = states[s] if s in states else st_scr[s]
                states[s] = finish_fn(cur, state)
            for s in range(n_seq):
                st_scr[s] = states[s]

        n_groups = n_chunks // group
        if n_groups == 1:
            run_group(0)
        else:
            lax.fori_loop(0, n_groups, lambda gi, carry: (run_group(gi), carry)[1], 0)

    for s in range(n_seq):
        st_scr[s] = s0f_ref[s, 0] if has_state else jnp.zeros((dk, dk), F32)
    sweep(fwd_local, fwd_finish, reverse=False)
    for s in range(n_seq):
        if emit_state:
            sf_out_ref[s, 0, 0] = st_scr[s]
        st_scr[s] = s0b_ref[s, 0] if has_state else jnp.zeros((dk, dk), F32)
    sweep(bwd_local, bwd_finish, reverse=True)
    if emit_state:
        for s in range(n_seq):
            sb_out_ref[s, 0, 0] = st_scr[s]


def _retention(proj, decay_logit, seq_len, n_seq, state_f=None, state_b=None, emit_state=False):
    t = proj.shape[0]
    d = D_MODEL
    dk = RET_DK
    heads = RET_HEADS
    rows = n_seq * seq_len
    has_state = state_f is not None

    def col(offset):
        return pl.BlockSpec((rows, dk), lambda b, h: (b, offset * heads + h))

    in_specs = [pl.BlockSpec(memory_space=pltpu.SMEM)] + [col(o) for o in range(5)]
    args = [decay_logit, proj, proj, proj, proj, proj]
    if has_state:
        st = pl.BlockSpec((n_seq, 1, dk, dk), lambda b, h: (b, h, 0, 0))
        in_specs += [st, st]
        args += [state_f, state_b]
    out_specs = [pl.BlockSpec((rows, dk), lambda b, h: (b, h))]
    out_shape = [jax.ShapeDtypeStruct((t, d), BF16)]
    if emit_state:
        n_batch = t // seq_len
        st_out = pl.BlockSpec((n_seq, 1, 1, dk, dk), lambda b, h: (b, 0, h, 0, 0))
        out_specs += [st_out, st_out]
        out_shape += [jax.ShapeDtypeStruct((n_batch, 1, heads, dk, dk), F32)] * 2
    return pl.pallas_call(
        functools.partial(_ret_kernel, n_seq=n_seq, seq_len=seq_len,
                          has_state=has_state, emit_state=emit_state, unroll=16),
        grid=(t // rows, heads),
        in_specs=in_specs,
        out_specs=out_specs,
        out_shape=out_shape,
        scratch_shapes=[pltpu.VMEM((rows, dk), F32), pltpu.VMEM((rows, RET_CHUNK), BF16),
                        pltpu.VMEM((n_seq, dk, dk), F32)],
        compiler_params=_cparams(2),
        name="retention_state" if emit_state else "retention",
    )(*args)


def kernel(x_prompt, x_sample, cache_na_k, cache_na_v, state_ret_fwd, state_ret_bwd, c, c_ctx,
           w_ada, b_ada, g_norm, na_w_qkv, na_w_o, na_rpb, ret_w_in, ret_w_o, ret_decay_logit,
           ffn_w_in, ffn_w_out):
    bp, seq, d = x_prompt.shape
    bs, n_tok, _ = x_sample.shape
    past = cache_na_k.shape[2]
    xp = x_prompt.reshape(bp * seq, d)
    xs = x_sample.reshape(bs * n_tok, d)

    cond = jnp.concatenate([c_ctx[None, :], c, jnp.zeros((8 - 1 - bs, d), F32)], axis=0)
    mod = _ada_modulation(cond, w_ada, b_ada).reshape(w_ada.shape[0], 8, 6, d)
    mod_p = [mod[l, 0:1] for l in range(2)]
    mod_s = [mod[l, 1:1 + bs] for l in range(2)]

    w_qkv = na_w_qkv[0].astype(BF16)
    w_na_o = na_w_o[0].astype(BF16)
    w_ret_in = ret_w_in[0].astype(BF16)
    w_ret_o = ret_w_o[0].astype(BF16)
    w_ffn_in = ffn_w_in.astype(BF16)
    w_ffn_out = _cast_bf16(ffn_w_out)

    qkv_p, k_p, v_p = _qkv_proj(xp, mod_p[0], g_norm[0], w_qkv, emit_kv=True)
    (qkv_s,) = _qkv_proj(xs, mod_s[0], g_norm[0], w_qkv, emit_kv=False)
    o_p = _ctx_attention(qkv_p, seq)
    bias = _na_bias_table(na_rpb[0], n_tok // GRID_W)
    ck = cache_na_k[:, 0].reshape(bs, past, d)
    cv = cache_na_v[:, 0].reshape(bs, past, d)
    o_s = _na_attention(qkv_s, ck, cv, bias, bs, n_tok)

    xp = _mix_out(o_p, w_na_o, xp, mod_p[0], g_norm[0])
    xs = _mix_out(o_s, w_na_o, xs, mod_s[0], g_norm[0])
    xp = _ffn(xp, mod_p[0], g_norm[0], w_ffn_in, w_ffn_out, 0)
    xs = _ffn(xs, mod_s[0], g_norm[0], w_ffn_in, w_ffn_out, 0)

    proj_p = _ret_in_proj(xp, mod_p[1], g_norm[1], w_ret_in)
    proj_s = _ret_in_proj(xs, mod_s[1], g_norm[1], w_ret_in, _rope_tables(n_tok, RET_DK), n_tok)
    y_p, sf, sb = _retention(proj_p, ret_decay_logit[0], seq, 8, emit_state=True)
    (y_s,) = _retention(proj_s, ret_decay_logit[0], n_tok, 1,
                        state_f=state_ret_fwd[:, 0], state_b=state_ret_bwd[:, 0])

    xp = _mix_out(y_p, w_ret_o, xp, mod_p[1], g_norm[1])
    xs = _mix_out(y_s, w_ret_o, xs, mod_s[1], g_norm[1])
    xp = _ffn(xp, mod_p[1], g_norm[1], w_ffn_in, w_ffn_out, 1)
    xs = _ffn(xs, mod_s[1], g_norm[1], w_ffn_in, w_ffn_out, 1)

    kv_shape = (bp, 1, seq, NA_HEADS, NA_HEAD_DIM)
    return (xp.reshape(bp, seq, d), xs.reshape(bs, n_tok, d),
            k_p.reshape(kv_shape), v_p.reshape(kv_shape),
            sf.astype(x_prompt.dtype), sb.astype(x_prompt.dtype))
```

```python
import functools

import jax
import jax.numpy as jnp
from jax import lax
from jax.experimental import pallas as pl
from jax.experimental.pallas import tpu as pltpu

D_MODEL = 2048
GRID_W = 64
NA_HEADS = 16
NA_HEAD_DIM = D_MODEL // NA_HEADS
NA_KH = 8
NA_KW = 16
RET_HEADS = 8
RET_DK = D_MODEL // RET_HEADS
RET_CHUNK = 128
FFN_HIDDEN = -(-8 * D_MODEL // (3 * 256)) * 256
ROPE_BASE = 10000.0
EPS = 1e-6

F32 = jnp.float32
BF16 = jnp.bfloat16
MASK_VALUE = -1e30

VMEM_LIMIT_BYTES = 56 * 1024 * 1024
FFN_VMEM_LIMIT_BYTES = 60 * 1024 * 1024
ROW_CHUNK = 256
NA_QROWS = 4
NA_QBLK = NA_QROWS * GRID_W
NA_BAND_BLKS = 3


def _cparams(n_axes, vmem_limit_bytes=VMEM_LIMIT_BYTES):
    return pltpu.CompilerParams(
        dimension_semantics=("arbitrary",) * n_axes,
        vmem_limit_bytes=vmem_limit_bytes)


def _silu(x):
    return x / (1.0 + jnp.exp(-x))


def _rms(x, g):
    ms = jnp.mean(x * x, axis=-1, keepdims=True)
    return x * lax.rsqrt(ms + EPS) * g


def _dot(a, b):
    return jnp.dot(a, b, preferred_element_type=F32)


def _dot_nt(a, b):
    return lax.dot_general(a, b, (((1,), (1,)), ((), ())), preferred_element_type=F32)


def _dot_tn(a, b):
    return lax.dot_general(a, b, (((0,), (0,)), ((), ())), preferred_element_type=F32)


def _ada_kernel(cond_ref, w_ref, b_ref, o_ref):
    a = _silu(cond_ref[...]).astype(BF16)
    o_ref[0] = _dot(a, w_ref[0].astype(BF16)) + b_ref[0]


def _ada_modulation(cond, w_ada, b_ada, tn=1024):
    depth, d, n = w_ada.shape
    rows = cond.shape[0]
    return pl.pallas_call(
        _ada_kernel,
        grid=(depth, n // tn),
        in_specs=[
            pl.BlockSpec((rows, d), lambda l, j: (0, 0)),
            pl.BlockSpec((1, d, tn), lambda l, j: (l, 0, j)),
            pl.BlockSpec((1, 1, tn), lambda l, j: (l, 0, j)),
        ],
        out_specs=pl.BlockSpec((1, rows, tn), lambda l, j: (l, 0, j)),
        out_shape=jax.ShapeDtypeStruct((depth, rows, n), F32),
        compiler_params=_cparams(2),
        name="ada_modulation",
    )(cond, w_ada, b_ada.reshape(depth, 1, n))


def _row_chunks(n_rows):
    return [slice(r * ROW_CHUNK, (r + 1) * ROW_CHUNK) for r in range(n_rows // ROW_CHUNK)]


def _qkv_kernel(x_ref, mod_ref, g_ref, w_ref, *rest, nq, nk, emit_kv):
    if emit_kv:
        o_ref, k_ref, v_ref, h_scr = rest
    else:
        o_ref, h_scr = rest
    j = pl.program_id(1)
    chunks = _row_chunks(x_ref.shape[0])
    q_scale = NA_HEAD_DIM ** -0.5

    @pl.when(j == 0)
    def _():
        g, sc, sh = g_ref[0:1, :], mod_ref[0, 1:2, :], mod_ref[0, 0:1, :]
        for sl in chunks:
            h = (_rms(x_ref[sl, :], g) * (1.0 + sc) + sh).astype(BF16)
            h_scr[sl, :] = h
            o_ref[sl, :] = (_dot(h, w_ref[...]) * q_scale).astype(BF16)

    @pl.when((j > 0) & (j < nq))
    def _():
        for sl in chunks:
            o_ref[sl, :] = (_dot(h_scr[sl, :], w_ref[...]) * q_scale).astype(BF16)

    @pl.when((j >= nq) & (j < nq + nk))
    def _():
        for sl in chunks:
            acc = _dot(h_scr[sl, :], w_ref[...])
            o_ref[sl, :] = acc.astype(BF16)
            if emit_kv:
                k_ref[sl, :] = acc

    @pl.when(j >= nq + nk)
    def _():
        for sl in chunks:
            acc = _dot(h_scr[sl, :], w_ref[...])
            o_ref[sl, :] = acc.astype(BF16)
            if emit_kv:
                v_ref[sl, :] = acc


def _qkv_proj(x, mod, g, w, *, emit_kv, tn=1024):
    t, d = x.shape
    tm = 1024
    n = w.shape[1]
    groups = mod.shape[0]
    tiles_per_group = t // groups // tm
    nq = d // tn
    nk = d // tn
    out_shape = [jax.ShapeDtypeStruct((t, n), BF16)]
    out_specs = [pl.BlockSpec((tm, tn), lambda i, j: (i, j))]
    if emit_kv:
        out_shape += [jax.ShapeDtypeStruct((t, d), F32)] * 2
        out_specs += [
            pl.BlockSpec((tm, tn), lambda i, j: (i, jnp.clip(j - nq, 0, nk - 1))),
            pl.BlockSpec((tm, tn), lambda i, j: (i, jnp.clip(j - nq - nk, 0, nk - 1))),
        ]
    return pl.pallas_call(
        functools.partial(_qkv_kernel, nq=nq, nk=nk, emit_kv=emit_kv),
        grid=(t // tm, n // tn),
        in_specs=[
            pl.BlockSpec((tm, d), lambda i, j: (i, 0)),
            pl.BlockSpec((1, 6, d), lambda i, j: (i // tiles_per_group, 0, 0)),
            pl.BlockSpec((4, d), lambda i, j: (0, 0)),
            pl.BlockSpec((d, tn), lambda i, j: (0, j)),
        ],
        out_specs=out_specs,
        out_shape=out_shape,
        scratch_shapes=[pltpu.VMEM((tm, d), BF16)],
        compiler_params=_cparams(2),
        name="qkv_proj_kv" if emit_kv else "qkv_proj",
    )(x, mod, g, w)


def _pipelined_heads(n_heads, n_blk, score_blk, value_blk, store):
    s_new = p_new = l_new = None
    for step in range(n_heads + 2):
        h_s, h_e, h_o = step, step - 1, step - 2
        s_cur, s_new = s_new, []
        p_cur, p_new = p_new, []
        l_cur, l_new = l_new, None
        exp_on = 0 <= h_e < n_heads
        out_on = 0 <= h_o < n_heads
        if exp_on:
            m = s_cur[0].max(axis=0, keepdims=True)
            for sj in s_cur[1:]:
                m = jnp.maximum(m, sj.max(axis=0, keepdims=True))
        o_t = None
        for jb in range(n_blk):
            if h_s < n_heads:
                s_new.append(score_blk(h_s, jb))
            if exp_on:
                p = jnp.exp(s_cur[jb] - m)
                part = p.sum(axis=0, keepdims=True)
                l_new = part if l_new is None else l_new + part
                p_new.append(p.astype(BF16))
            if out_on:
                part = _dot_tn(value_blk(h_o, jb), p_cur[jb])
                o_t = part if o_t is None else o_t + part
        if out_on:
            store(h_o, (o_t * (1.0 / l_cur)).T)


def _ctx_attn_kernel(q_ref, k_ref, v_ref, o_ref):
    def cols(h):
        return slice(h * NA_HEAD_DIM, (h + 1) * NA_HEAD_DIM)

    def scores(h):
        return _dot_nt(q_ref[:, cols(h)], k_ref[:, cols(h)])

    s_next = scores(0)
    for h in range(NA_HEADS):
        s = s_next
        if h + 1 < NA_HEADS:
            s_next = scores(h + 1)
        m = jnp.max(s, axis=-1, keepdims=True)
        p = jnp.exp(s - m)
        l = jnp.sum(p, axis=-1, keepdims=True)
        o = _dot(p.astype(BF16), v_ref[:, cols(h)])
        o_ref[:, cols(h)] = (o * (1.0 / l)).astype(BF16)


def _ctx_attention(qkv, seq):
    t = qkv.shape[0]
    d = D_MODEL
    return pl.pallas_call(
        _ctx_attn_kernel,
        grid=(t // seq,),
        in_specs=[
            pl.BlockSpec((seq, d), lambda b: (b, 0)),
            pl.BlockSpec((seq, d), lambda b: (b, 1)),
            pl.BlockSpec((seq, d), lambda b: (b, 2)),
        ],
        out_specs=pl.BlockSpec((seq, d), lambda b: (b, 0)),
        out_shape=jax.ShapeDtypeStruct((t, d), BF16),
        compiler_params=_cparams(1),
        name="ctx_attention",
    )(qkv, qkv, qkv)


def _band_start_blk(rb, n_rb):
    return jnp.clip(rb - 1, 0, n_rb - NA_BAND_BLKS)


def _bias_tile_plan(n_grid_rows):
    n_rb = n_grid_rows // NA_QROWS
    plans = []
    for rb in (0, 1, n_rb - 1):
        r0 = rb * NA_QROWS
        a = min(max(rb - 1, 0), n_rb - NA_BAND_BLKS) * NA_QROWS
        rows = []
        for j in range(NA_BAND_BLKS * NA_QROWS):
            kr = a + j
            pairs = []
            for ip in range(NA_QROWS // 2):
                drs = []
                for i in (2 * ip, 2 * ip + 1):
                    r = r0 + i
                    rs = min(max(r - NA_KH // 2, 0), n_grid_rows - NA_KH)
                    drs.append(kr - r + NA_KH - 1 if rs <= kr < rs + NA_KH else None)
                pairs.append(tuple(drs))
            rows.append(pairs)
        plans.append(rows)
    return plans


def _na_bias_kernel(rpb_ref, o_ref, *, plan):
    h = pl.program_id(0)
    n_dr = 2 * NA_KH - 1
    n_dc = 2 * NA_KW - 1
    shape = (GRID_W, 2 * GRID_W)
    kc = lax.broadcasted_iota(jnp.int32, shape, 0)
    lane = lax.broadcasted_iota(jnp.int32, shape, 1)
    qc = lane & (GRID_W - 1)
    right = lane >= GRID_W
    diff = kc - qc
    cstart = jnp.clip(qc - NA_KW // 2, 0, GRID_W - NA_KW)
    in_window = (kc >= cstart) & (kc < cstart + NA_KW)
    masked = jnp.full(shape, MASK_VALUE, F32)

    used = sorted({dr for rows in plan for pairs in rows for pr in pairs for dr in pr if dr is not None})
    tiles = {dr: masked for dr in used}
    for dc in range(n_dc):
        hit = in_window & (diff == dc - (NA_KW - 1))
        for dr in used:
            tiles[dr] = jnp.where(hit, rpb_ref[h * (n_dr * n_dc) + dr * n_dc + dc], tiles[dr])
    tiles[None] = masked

    for ty, rows in enumerate(plan):
        for j, pairs in enumerate(rows):
            for ip, (dl, dr) in enumerate(pairs):
                tile = tiles[dl] if dl == dr else jnp.where(right, tiles[dr], tiles[dl])
                o_ref[ty, 0, j * GRID_W:(j + 1) * GRID_W, ip * 2 * GRID_W:(ip + 1) * 2 * GRID_W] = tile


def _na_bias_table(rpb, n_grid_rows):
    heads = rpb.shape[0]
    band = NA_BAND_BLKS * NA_QBLK
    plan = _bias_tile_plan(n_grid_rows)
    return pl.pallas_call(
        functools.partial(_na_bias_kernel, plan=plan),
        grid=(heads,),
        in_specs=[pl.BlockSpec(memory_space=pltpu.SMEM)],
        out_specs=pl.BlockSpec((3, 1, band, NA_QBLK), lambda h: (0, h, 0, 0)),
        out_shape=jax.ShapeDtypeStruct((3, heads, band, NA_QBLK), F32),
        compiler_params=_cparams(1),
        name="na_bias_table",
    )(rpb.reshape(-1))


def _na_attn_kernel(q_ref, k0_ref, k1_ref, k2_ref, v0_ref, v1_ref, v2_ref,
                    ck_ref, cv_ref, bias_ref, o_ref):
    dh = NA_HEAD_DIM
    n_heads = q_ref.shape[1] // dh

    n_blk = NA_BAND_BLKS + 1
    k_refs = (k0_ref, k1_ref, k2_ref)
    v_refs = (v0_ref, v1_ref, v2_ref)

    def head_cols(hh):
        return slice(hh * dh, (hh + 1) * dh)

    def score_blk(hh, jb):
        sl = head_cols(hh)
        if jb < NA_BAND_BLKS:
            return (_dot_nt(k_refs[jb][:, sl], q_ref[:, sl])
                    + bias_ref[0, hh, jb * NA_QBLK:(jb + 1) * NA_QBLK, :])
        return _dot_nt(ck_ref[0, :, sl].astype(BF16), q_ref[:, sl])

    def value_blk(hh, jb):
        sl = head_cols(hh)
        return v_refs[jb][:, sl] if jb < NA_BAND_BLKS else cv_ref[0, :, sl].astype(BF16)

    def store(hh, o):
        o_ref[:, head_cols(hh)] = o.astype(BF16)

    _pipelined_heads(n_heads, n_blk, score_blk, value_blk, store)


def _na_attention(qkv, ck, cv, bias, batch, n_tok, heads_per_step=8):
    d = D_MODEL
    width = heads_per_step * NA_HEAD_DIM
    n_hg = NA_HEADS // heads_per_step
    n_rb = n_tok // NA_QBLK
    past = ck.shape[1]

    def qmap(hg, b, rb):
        return (b * n_rb + rb, hg)

    def band_map(col0, off):
        def f(hg, b, rb):
            return (b * n_rb + _band_start_blk(rb, n_rb) + off, col0 + hg)
        return f

    def bias_map(hg, b, rb):
        ty = jnp.where(rb == 0, 0, jnp.where(rb == n_rb - 1, 2, 1))
        return (ty, hg, 0, 0)

    blk = pl.BlockSpec((NA_QBLK, width), qmap)
    in_specs = [blk]
    in_specs += [pl.BlockSpec((NA_QBLK, width), band_map(n_hg, off)) for off in range(NA_BAND_BLKS)]
    in_specs += [pl.BlockSpec((NA_QBLK, width), band_map(2 * n_hg, off)) for off in range(NA_BAND_BLKS)]
    in_specs += [pl.BlockSpec((1, past, width), lambda hg, b, rb: (b, 0, hg))] * 2
    in_specs += [pl.BlockSpec((1, heads_per_step, NA_BAND_BLKS * NA_QBLK, NA_QBLK), bias_map)]
    return pl.pallas_call(
        _na_attn_kernel,
        grid=(n_hg, batch, n_rb),
        in_specs=in_specs,
        out_specs=blk,
        out_shape=jax.ShapeDtypeStruct((batch * n_tok, d), BF16),
        compiler_params=_cparams(3),
        name="na_attention",
    )(qkv, qkv, qkv, qkv, qkv, qkv, qkv, ck, cv, bias)


def _mix_out_kernel(o_ref, w_ref, x_ref, mod_ref, g_ref, x1_ref):
    ga = mod_ref[0, 2:3, :]
    chunks = _row_chunks(o_ref.shape[0])
    y_next = _dot(o_ref[chunks[0], :], w_ref[...])
    for idx, sl in enumerate(chunks):
        y = y_next
        if idx + 1 < len(chunks):
            y_next = _dot(o_ref[chunks[idx + 1], :], w_ref[...])
        x1_ref[sl, :] = x_ref[sl, :] + ga * _rms(y, g_ref[1:2, :])


def _mix_out(o, w, x, mod, g, tm=512):
    t, d = x.shape
    groups = mod.shape[0]
    tiles_per_group = t // groups // tm
    return pl.pallas_call(
        _mix_out_kernel,
        grid=(t // tm,),
        in_specs=[
            pl.BlockSpec((tm, d), lambda i: (i, 0)),
            pl.BlockSpec((d, d), lambda i: (0, 0)),
            pl.BlockSpec((tm, d), lambda i: (i, 0)),
            pl.BlockSpec((1, 6, d), lambda i: (i // tiles_per_group, 0, 0)),
            pl.BlockSpec((4, d), lambda i: (0, 0)),
        ],
        out_specs=pl.BlockSpec((tm, d), lambda i: (i, 0)),
        out_shape=jax.ShapeDtypeStruct((t, d), F32),
        compiler_params=_cparams(1),
        name="mix_out",
    )(o, w, x, mod, g)


def _ffn_kernel(x_ref, mod_ref, g_ref, wg_ref, wu_ref, wo_ref, x2_ref, h_scr):
    c = pl.program_id(1)
    last = pl.num_programs(1) - 1
    row_chunks = _row_chunks(x_ref.shape[0])

    def hidden(h):
        return (_silu(_dot(h, wg_ref[0])) * _dot(h, wu_ref[0])).astype(BF16)

    def over_chunks(get_h, emit):
        a_next = hidden(get_h(row_chunks[0]))
        w_down = wo_ref[0].astype(BF16)
        for idx, sl in enumerate(row_chunks):
            a = a_next
            if idx + 1 < len(row_chunks):
                a_next = hidden(get_h(row_chunks[idx + 1]))
            emit(sl, _dot(a, w_down))

    @pl.when(c == 0)
    def _():
        g, sh, sc = g_ref[2:3, :], mod_ref[0, 3:4, :], mod_ref[0, 4:5, :]

        def normed(sl):
            h = (_rms(x_ref[sl, :], g) * (1.0 + sc) + sh).astype(BF16)
            h_scr[sl, :] = h
            return h

        def assign(sl, part):
            x2_ref[sl, :] = part

        over_chunks(normed, assign)

    @pl.when((c > 0) & (c < last))
    def _():
        def accumulate(sl, part):
            x2_ref[sl, :] += part

        over_chunks(lambda sl: h_scr[sl, :], accumulate)

    @pl.when(c == last)
    def _():
        ga = mod_ref[0, 5:6, :]

        def finish(sl, part):
            y = x2_ref[sl, :] + part
            x2_ref[sl, :] = x_ref[sl, :] + ga * _rms(y, g_ref[3:4, :])

        over_chunks(lambda sl: h_scr[sl, :], finish)


def _ffn(x, mod, g, w_in, w_out, layer, tm=1024, tf=512):
    t, d = x.shape
    hidden = w_out.shape[1]
    n_c = hidden // tf
    groups = mod.shape[0]
    tiles_per_group = t // groups // tm
    row = pl.BlockSpec((tm, d), lambda i, c: (i, 0))
    return pl.pallas_call(
        _ffn_kernel,
        grid=(t // tm, n_c),
        in_specs=[
            row,
            pl.BlockSpec((1, 6, d), lambda i, c: (i // tiles_per_group, 0, 0)),
            pl.BlockSpec((4, d), lambda i, c: (0, 0)),
            pl.BlockSpec((1, d, tf), lambda i, c: (layer, 0, c)),
            pl.BlockSpec((1, d, tf), lambda i, c: (layer, 0, n_c + c)),
            pl.BlockSpec((1, tf, d), lambda i, c: (layer, c, 0)),
        ],
        out_specs=row,
        out_shape=jax.ShapeDtypeStruct((t, d), F32),
        scratch_shapes=[pltpu.VMEM((tm, d), BF16)],
        compiler_params=_cparams(2, FFN_VMEM_LIMIT_BYTES),
        name="ffn",
    )(x, mod, g, w_in, w_in, w_out)


def _ret_in_kernel(x_ref, mod_ref, g_ref, w_ref, *rest, rope, n_qk, n_q, n_v):
    if rope:
        cos_ref, sin_ref, o_ref, h_scr = rest
    else:
        o_ref, h_scr = rest
    j = pl.program_id(1)
    tn = w_ref.shape[1]
    half = RET_DK // 2
    chunks = _row_chunks(x_ref.shape[0])

    def store_qk(sl, acc, scale):
        for hh in range(tn // RET_DK):
            cols = slice(hh * RET_DK, (hh + 1) * RET_DK)
            x = acc[:, cols]
            if rope:
                rot = jnp.concatenate(
                    [pltpu.roll(x[:, :half], half // 2, 1), pltpu.roll(x[:, half:], half // 2, 1)], axis=1)
                x = x * cos_ref[sl, :] + rot * sin_ref[sl, :]
            o_ref[sl, cols] = (x * scale).astype(BF16)

    @pl.when(j == 0)
    def _():
        g, sc, sh = g_ref[0:1, :], mod_ref[0, 1:2, :], mod_ref[0, 0:1, :]
        for sl in chunks:
            h = (_rms(x_ref[sl, :], g) * (1.0 + sc) + sh).astype(BF16)
            h_scr[sl, :] = h
            store_qk(sl, _dot(h, w_ref[...]), RET_DK ** -0.5)

    @pl.when((j > 0) & (j < n_qk))
    def _():
        scale = jnp.where(j < n_q, RET_DK ** -0.5, 1.0).astype(F32)
        for sl in chunks:
            store_qk(sl, _dot(h_scr[sl, :], w_ref[...]), scale)

    @pl.when((j >= n_qk) & (j < n_qk + n_v))
    def _():
        for sl in chunks:
            o_ref[sl, :] = _dot(h_scr[sl, :], w_ref[...]).astype(BF16)

    @pl.when(j >= n_qk + n_v)
    def _():
        for sl in chunks:
            o_ref[sl, :] = _silu(_dot(h_scr[sl, :], w_ref[...])).astype(BF16)


def _ret_in_proj(x, mod, g, w, rope_tables=None, seq_len=None, tm=1024, tn=1024):
    t, d = x.shape
    n = w.shape[1]
    rope = rope_tables is not None
    groups = mod.shape[0]
    tiles_per_group = t // groups // tm
    in_specs = [
        pl.BlockSpec((tm, d), lambda i, j: (i, 0)),
        pl.BlockSpec((1, 6, d), lambda i, j: (i // tiles_per_group, 0, 0)),
        pl.BlockSpec((4, d), lambda i, j: (0, 0)),
        pl.BlockSpec((d, tn), lambda i, j: (0, j)),
    ]
    args = [x, mod, g, w]
    if rope:
        tiles_per_seq = seq_len // tm
        tab = pl.BlockSpec((tm, RET_DK), lambda i, j: (i % tiles_per_seq, 0))
        in_specs += [tab, tab]
        args += list(rope_tables)
    return pl.pallas_call(
        functools.partial(_ret_in_kernel, rope=rope, n_qk=2 * d // tn, n_q=d // tn, n_v=d // tn),
        grid=(t // tm, n // tn),
        in_specs=in_specs,
        out_specs=pl.BlockSpec((tm, tn), lambda i, j: (i, j)),
        out_shape=jax.ShapeDtypeStruct((t, n), BF16),
        scratch_shapes=[pltpu.VMEM((tm, d), BF16)],
        compiler_params=_cparams(2),
        name="ret_in_rope" if rope else "ret_in",
    )(*args)


def _rope_tables(n_tokens, dim):
    n_rows = n_tokens // GRID_W
    half = dim // 2
    quarter = dim // 4
    inv = jnp.power(ROPE_BASE, -jnp.arange(0, half, 2, dtype=F32) / half)
    ang_r = jnp.arange(n_rows, dtype=F32)[:, None] * inv[None, :]
    ang_c = jnp.arange(GRID_W, dtype=F32)[:, None] * inv[None, :]

    def table(fn, lo_sign):
        by_row = jnp.broadcast_to(fn(ang_r)[:, None, :], (n_rows, GRID_W, quarter))
        by_col = jnp.broadcast_to(fn(ang_c)[None, :, :], (n_rows, GRID_W, quarter))
        full = jnp.concatenate([lo_sign * by_row, by_row, lo_sign * by_col, by_col], axis=-1)
        return full.reshape(n_tokens, dim)

    return table(jnp.cos, 1.0), table(jnp.sin, -1.0)


def _log_sigmoid(x):
    y = -x
    return -(jnp.maximum(y, 0.0) + jnp.log1p(jnp.exp(-jnp.abs(y))))


def _group_norm(o):
    mu = jnp.mean(o, axis=-1, keepdims=True)
    dev = o - mu
    var = jnp.mean(dev * dev, axis=-1, keepdims=True)
    return dev * lax.rsqrt(var + EPS)


def _ret_kernel(dl_ref, q_ref, k_ref, v_ref, gf_ref, gb_ref, *rest,
                n_seq, seq_len, has_state, emit_state, unroll):
    rest = list(rest)
    if has_state:
        s0f_ref, s0b_ref = rest[:2]
        rest = rest[2:]
    y_ref = rest.pop(0)
    if emit_state:
        sf_out_ref, sb_out_ref = rest[:2]
        rest = rest[2:]
    of_scr, pb_scr, st_scr = rest

    head = pl.program_id(1)
    C = RET_CHUNK
    dk = RET_DK
    n_chunks = seq_len // C

    def lg(shape, direction):
        return _log_sigmoid(jnp.full(shape, dl_ref[direction, head], F32))

    ii = lax.broadcasted_iota(jnp.int32, (C, C), 0)
    jj = lax.broadcasted_iota(jnp.int32, (C, C), 1)
    causal = ii >= jj
    anti = jj >= ii
    dist = jnp.abs(ii - jj).astype(F32)
    decay_f = jnp.where(causal, jnp.exp(jnp.where(causal, dist, 0.0) * lg((C, C), 0)), 0.0)
    decay_b = jnp.where(anti, jnp.exp(jnp.where(anti, dist, 0.0) * lg((C, C), 1)), 0.0)
    pos = lax.broadcasted_iota(jnp.int32, (C, dk), 0).astype(F32)
    lgf = lg((C, dk), 0)
    lgb = lg((C, dk), 1)
    qd_f = jnp.exp((pos + 1.0) * lgf)
    kd_f = jnp.exp((C - 1.0 - pos) * lgf)
    qd_b = jnp.exp((C - pos) * lgb)
    kd_b = jnp.exp(pos * lgb)
    cd_f = jnp.exp(C * lg((1, dk), 0))
    cd_b = jnp.exp(C * lg((1, dk), 1))

    def chunk_rows(s, t):
        return pl.ds(pl.multiple_of(s * seq_len + t * C, C), C)

    def key_value(k, v, kd):
        return _dot_tn((k.astype(F32) * kd).astype(BF16), v)

    def fwd_local(s, t):
        rows = chunk_rows(s, t)
        q, k, v = q_ref[rows, :], k_ref[rows, :], v_ref[rows, :]
        raw = _dot_nt(q, k)
        pb_scr[rows, :] = (raw * decay_b).astype(BF16)
        return rows, q, _dot((raw * decay_f).astype(BF16), v), key_value(k, v, kd_f)

    def fwd_finish(local, state):
        rows, q, inner, kv = local
        of_scr[rows, :] = _group_norm(inner + _dot(q, state.astype(BF16)) * qd_f)
        return state * cd_f + kv

    def bwd_local(s, t):
        rows = chunk_rows(s, t)
        q, k, v = q_ref[rows, :], k_ref[rows, :], v_ref[rows, :]
        return rows, q, _dot(pb_scr[rows, :], v), key_value(k, v, kd_b)

    def bwd_finish(local, state):
        rows, q, inner, kv = local
        o = inner + _dot(q, state.astype(BF16)) * qd_b
        y = (gf_ref[rows, :].astype(F32) * of_scr[rows, :]
             + gb_ref[rows, :].astype(F32) * _group_norm(o))
        y_ref[rows, :] = y.astype(BF16)
        return state * cd_b + kv

    def sweep(local_fn, finish_fn, reverse):
        group = min(unroll, n_chunks)

        def run_group(gi):
            items = [(s, gi * group + c) for s in range(n_seq) for c in range(group)]
            states = {}

            def local_at(item):
                s, t = item
                return local_fn(s, n_chunks - 1 - t if reverse else t)

            nxt = local_at(items[0])
            for idx, (s, t) in enumerate(items):
                cur = nxt
                if idx + 1 < len(items):
                    nxt = local_at(items[idx + 1])
                state = states[s] if s in states else st_scr[s]
                states[s] = finish_fn(cur, state)
            for s in range(n_seq):
                st_scr[s] = states[s]

        n_groups = n_chunks // group
        if n_groups == 1:
            run_group(0)
        else:
            lax.fori_loop(0, n_groups, lambda gi, carry: (run_group(gi), carry)[1], 0)

    for s in range(n_seq):
        st_scr[s] = s0f_ref[s, 0] if has_state else jnp.zeros((dk, dk), F32)
    sweep(fwd_local, fwd_finish, reverse=False)
    for s in range(n_seq):
        if emit_state:
            sf_out_ref[s, 0, 0] = st_scr[s]
        st_scr[s] = s0b_ref[s, 0] if has_state else jnp.zeros((dk, dk), F32)
    sweep(bwd_local, bwd_finish, reverse=True)
    if emit_state:
        for s in range(n_seq):
            sb_out_ref[s, 0, 0] = st_scr[s]


def _retention(proj, decay_logit, seq_len, n_seq, state_f=None, state_b=None, emit_state=False):
    t = proj.shape[0]
    d = D_MODEL
    dk = RET_DK
    heads = RET_HEADS
    rows = n_seq * seq_len
    has_state = state_f is not None

    def col(offset):
        return pl.BlockSpec((rows, dk), lambda b, h: (b, offset * heads + h))

    in_specs = [pl.BlockSpec(memory_space=pltpu.SMEM)] + [col(o) for o in range(5)]
    args = [decay_logit, proj, proj, proj, proj, proj]
    if has_state:
        st = pl.BlockSpec((n_seq, 1, dk, dk), lambda b, h: (b, h, 0, 0))
        in_specs += [st, st]
        args += [state_f, state_b]
    out_specs = [pl.BlockSpec((rows, dk), lambda b, h: (b, h))]
    out_shape = [jax.ShapeDtypeStruct((t, d), BF16)]
    if emit_state:
        n_batch = t // seq_len
        st_out = pl.BlockSpec((n_seq, 1, 1, dk, dk), lambda b, h: (b, 0, h, 0, 0))
        out_specs += [st_out, st_out]
        out_shape += [jax.ShapeDtypeStruct((n_batch, 1, heads, dk, dk), F32)] * 2
    return pl.pallas_call(
        functools.partial(_ret_kernel, n_seq=n_seq, seq_len=seq_len,
                          has_state=has_state, emit_state=emit_state, unroll=16),
        grid=(t // rows, heads),
        in_specs=in_specs,
        out_specs=out_specs,
        out_shape=out_shape,
        scratch_shapes=[pltpu.VMEM((rows, dk), F32), pltpu.VMEM((rows, RET_CHUNK), BF16),
                        pltpu.VMEM((n_seq, dk, dk), F32)],
        compiler_params=_cparams(2),
        name="retention_state" if emit_state else "retention",
    )(*args)


def kernel(x_prompt, x_sample, cache_na_k, cache_na_v, state_ret_fwd, state_ret_bwd, c, c_ctx,
           w_ada, b_ada, g_norm, na_w_qkv, na_w_o, na_rpb, ret_w_in, ret_w_o, ret_decay_logit,
           ffn_w_in, ffn_w_out):
    bp, seq, d = x_prompt.shape
    bs, n_tok, _ = x_sample.shape
    past = cache_na_k.shape[2]
    xp = x_prompt.reshape(bp * seq, d)
    xs = x_sample.reshape(bs * n_tok, d)

    cond = jnp.concatenate([c_ctx[None, :], c, jnp.zeros((8 - 1 - bs, d), F32)], axis=0)
    mod = _ada_modulation(cond, w_ada, b_ada).reshape(w_ada.shape[0], 8, 6, d)
    mod_p = [mod[l, 0:1] for l in range(2)]
    mod_s = [mod[l, 1:1 + bs] for l in range(2)]

    w_qkv = na_w_qkv[0].astype(BF16)
    w_na_o = na_w_o[0].astype(BF16)
    w_ret_in = ret_w_in[0].astype(BF16)
    w_ret_o = ret_w_o[0].astype(BF16)
    w_ffn_in = ffn_w_in.astype(BF16)
    w_ffn_out = ffn_w_out

    qkv_p, k_p, v_p = _qkv_proj(xp, mod_p[0], g_norm[0], w_qkv, emit_kv=True)
    (qkv_s,) = _qkv_proj(xs, mod_s[0], g_norm[0], w_qkv, emit_kv=False)
    o_p = _ctx_attention(qkv_p, seq)
    bias = _na_bias_table(na_rpb[0], n_tok // GRID_W)
    ck = cache_na_k[:, 0].reshape(bs, past, d)
    cv = cache_na_v[:, 0].reshape(bs, past, d)
    o_s = _na_attention(qkv_s, ck, cv, bias, bs, n_tok)

    xp = _mix_out(o_p, w_na_o, xp, mod_p[0], g_norm[0])
    xs = _mix_out(o_s, w_na_o, xs, mod_s[0], g_norm[0])
    xp = _ffn(xp, mod_p[0], g_norm[0], w_ffn_in, w_ffn_out, 0)
    xs = _ffn(xs, mod_s[0], g_norm[0], w_ffn_in, w_ffn_out, 0)

    proj_p = _ret_in_proj(xp, mod_p[1], g_norm[1], w_ret_in)
    proj_s = _ret_in_proj(xs, mod_s[1], g_norm[1], w_ret_in, _rope_tables(n_tok, RET_DK), n_tok)
    y_p, sf, sb = _retention(proj_p, ret_decay_logit[0], seq, 8, emit_state=True)
    (y_s,) = _retention(proj_s, ret_decay_logit[0], n_tok, 1,
                        state_f=state_ret_fwd[:, 0], state_b=state_ret_bwd[:, 0])

    xp = _mix_out(y_p, w_ret_o, xp, mod_p[1], g_norm[1])
    xs = _mix_out(y_s, w_ret_o, xs, mod_s[1], g_norm[1])
    xp = _ffn(xp, mod_p[1], g_norm[1], w_ffn_in, w_ffn_out, 1)
    xs = _ffn(xs, mod_s[1], g_norm[1], w_ffn_in, w_ffn_out, 1)

    kv_shape = (bp, 1, seq, NA_HEADS, NA_HEAD_DIM)
    return (xp.reshape(bp, seq, d), xs.reshape(bs, n_tok, d),
            k_p.reshape(kv_shape), v_p.reshape(kv_shape),
            sf.astype(x_prompt.dtype), sb.astype(x_prompt.dtype))
```

```python
import functools

import jax
import jax.numpy as jnp
from jax import lax
from jax.experimental import pallas as pl
from jax.experimental.pallas import tpu as pltpu

D_MODEL = 2048
GRID_W = 64
NA_HEADS = 16
NA_HEAD_DIM = D_MODEL // NA_HEADS
NA_KH = 8
NA_KW = 16
RET_HEADS = 8
RET_DK = D_MODEL // RET_HEADS
RET_CHUNK = 128
FFN_HIDDEN = -(-8 * D_MODEL // (3 * 256)) * 256
ROPE_BASE = 10000.0
EPS = 1e-6

F32 = jnp.float32
BF16 = jnp.bfloat16
MASK_VALUE = -1e30

VMEM_LIMIT_BYTES = 56 * 1024 * 1024
FFN_VMEM_LIMIT_BYTES = 60 * 1024 * 1024
ROW_CHUNK = 256
NA_QROWS = 4
NA_QBLK = NA_QROWS * GRID_W
NA_BAND_BLKS = 3


def _cparams(n_axes, vmem_limit_bytes=VMEM_LIMIT_BYTES):
    return pltpu.CompilerParams(
        dimension_semantics=("arbitrary",) * n_axes,
        vmem_limit_bytes=vmem_limit_bytes)


def _silu(x):
    return x / (1.0 + jnp.exp(-x))


def _rms(x, g):
    ms = jnp.mean(x * x, axis=-1, keepdims=True)
    return x * lax.rsqrt(ms + EPS) * g


def _dot(a, b):
    return jnp.dot(a, b, preferred_element_type=F32)


def _dot_nt(a, b):
    return lax.dot_general(a, b, (((1,), (1,)), ((), ())), preferred_element_type=F32)


def _dot_tn(a, b):
    return lax.dot_general(a, b, (((0,), (0,)), ((), ())), preferred_element_type=F32)


def _ada_kernel(cond_ref, w_ref, b_ref, o_ref):
    a = _silu(cond_ref[...]).astype(BF16)
    o_ref[0] = _dot(a, w_ref[0].astype(BF16)) + b_ref[0]


def _ada_modulation(cond, w_ada, b_ada, tn=1024):
    depth, d, n = w_ada.shape
    rows = cond.shape[0]
    return pl.pallas_call(
        _ada_kernel,
        grid=(depth, n // tn),
        in_specs=[
            pl.BlockSpec((rows, d), lambda l, j: (0, 0)),
            pl.BlockSpec((1, d, tn), lambda l, j: (l, 0, j)),
            pl.BlockSpec((1, 1, tn), lambda l, j: (l, 0, j)),
        ],
        out_specs=pl.BlockSpec((1, rows, tn), lambda l, j: (l, 0, j)),
        out_shape=jax.ShapeDtypeStruct((depth, rows, n), F32),
        compiler_params=_cparams(2),
        name="ada_modulation",
    )(cond, w_ada, b_ada.reshape(depth, 1, n))


def _row_chunks(n_rows):
    return [slice(r * ROW_CHUNK, (r + 1) * ROW_CHUNK) for r in range(n_rows // ROW_CHUNK)]


def _qkv_kernel(x_ref, mod_ref, g_ref, w_ref, *rest, nq, nk, emit_kv):
    if emit_kv:
        o_ref, k_ref, v_ref, h_scr = rest
    else:
        o_ref, h_scr = rest
    j = pl.program_id(1)
    chunks = _row_chunks(x_ref.shape[0])
    q_scale = NA_HEAD_DIM ** -0.5

    @pl.when(j == 0)
    def _():
        g, sc, sh = g_ref[0:1, :], mod_ref[0, 1:2, :], mod_ref[0, 0:1, :]
        for sl in chunks:
            h = (_rms(x_ref[sl, :], g) * (1.0 + sc) + sh).astype(BF16)
            h_scr[sl, :] = h
            o_ref[sl, :] = (_dot(h, w_ref[...]) * q_scale).astype(BF16)

    @pl.when((j > 0) & (j < nq))
    def _():
        for sl in chunks:
            o_ref[sl, :] = (_dot(h_scr[sl, :], w_ref[...]) * q_scale).astype(BF16)

    @pl.when((j >= nq) & (j < nq + nk))
    def _():
        for sl in chunks:
            acc = _dot(h_scr[sl, :], w_ref[...])
            o_ref[sl, :] = acc.astype(BF16)
            if emit_kv:
                k_ref[sl, :] = acc

    @pl.when(j >= nq + nk)
    def _():
        for sl in chunks:
            acc = _dot(h_scr[sl, :], w_ref[...])
            o_ref[sl, :] = acc.astype(BF16)
            if emit_kv:
                v_ref[sl, :] = acc


def _qkv_proj(x, mod, g, w, *, emit_kv):
    t, d = x.shape
    tm = 1024
    tn = 1024 if emit_kv else 2048
    n = w.shape[1]
    groups = mod.shape[0]
    tiles_per_group = t // groups // tm
    nq = d // tn
    nk = d // tn
    out_shape = [jax.ShapeDtypeStruct((t, n), BF16)]
    out_specs = [pl.BlockSpec((tm, tn), lambda i, j: (i, j))]
    if emit_kv:
        out_shape += [jax.ShapeDtypeStruct((t, d), F32)] * 2
        out_specs += [
            pl.BlockSpec((tm, tn), lambda i, j: (i, jnp.clip(j - nq, 0, nk - 1))),
            pl.BlockSpec((tm, tn), lambda i, j: (i, jnp.clip(j - nq - nk, 0, nk - 1))),
        ]
    return pl.pallas_call(
        functools.partial(_qkv_kernel, nq=nq, nk=nk, emit_kv=emit_kv),
        grid=(t // tm, n // tn),
        in_specs=[
            pl.BlockSpec((tm, d), lambda i, j: (i, 0)),
            pl.BlockSpec((1, 6, d), lambda i, j: (i // tiles_per_group, 0, 0)),
            pl.BlockSpec((4, d), lambda i, j: (0, 0)),
            pl.BlockSpec((d, tn), lambda i, j: (0, j)),
        ],
        out_specs=out_specs,
        out_shape=out_shape,
        scratch_shapes=[pltpu.VMEM((tm, d), BF16)],
        compiler_params=_cparams(2),
        name="qkv_proj_kv" if emit_kv else "qkv_proj",
    )(x, mod, g, w)


def _pipelined_heads(n_heads, n_blk, score_blk, value_blk, store):
    s_new = p_new = l_new = None
    for step in range(n_heads + 2):
        h_s, h_e, h_o = step, step - 1, step - 2
        s_cur, s_new = s_new, []
        p_cur, p_new = p_new, []
        l_cur, l_new = l_new, None
        exp_on = 0 <= h_e < n_heads
        out_on = 0 <= h_o < n_heads
        if exp_on:
            m = s_cur[0].max(axis=0, keepdims=True)
            for sj in s_cur[1:]:
                m = jnp.maximum(m, sj.max(axis=0, keepdims=True))
        o_t = None
        for jb in range(n_blk):
            if h_s < n_heads:
                s_new.append(score_blk(h_s, jb))
            if exp_on:
                p = jnp.exp(s_cur[jb] - m)
                part = p.sum(axis=0, keepdims=True)
                l_new = part if l_new is None else l_new + part
                p_new.append(p.astype(BF16))
            if out_on:
                part = _dot_tn(value_blk(h_o, jb), p_cur[jb])
                o_t = part if o_t is None else o_t + part
        if out_on:
            store(h_o, (o_t * (1.0 / l_cur)).T)


def _ctx_attn_kernel(q_ref, k_ref, v_ref, o_ref):
    def cols(h):
        return slice(h * NA_HEAD_DIM, (h + 1) * NA_HEAD_DIM)

    def scores(h):
        return _dot_nt(q_ref[:, cols(h)], k_ref[:, cols(h)])

    s_next = scores(0)
    for h in range(NA_HEADS):
        s = s_next
        if h + 1 < NA_HEADS:
            s_next = scores(h + 1)
        m = jnp.max(s, axis=-1, keepdims=True)
        p = jnp.exp(s - m)
        l = jnp.sum(p, axis=-1, keepdims=True)
        o = _dot(p.astype(BF16), v_ref[:, cols(h)])
        o_ref[:, cols(h)] = (o * (1.0 / l)).astype(BF16)


def _ctx_attention(qkv, seq):
    t = qkv.shape[0]
    d = D_MODEL
    return pl.pallas_call(
        _ctx_attn_kernel,
        grid=(t // seq,),
        in_specs=[
            pl.BlockSpec((seq, d), lambda b: (b, 0)),
            pl.BlockSpec((seq, d), lambda b: (b, 1)),
            pl.BlockSpec((seq, d), lambda b: (b, 2)),
        ],
        out_specs=pl.BlockSpec((seq, d), lambda b: (b, 0)),
        out_shape=jax.ShapeDtypeStruct((t, d), BF16),
        compiler_params=_cparams(1),
        name="ctx_attention",
    )(qkv, qkv, qkv)


def _band_start_blk(rb, n_rb):
    return jnp.clip(rb - 1, 0, n_rb - NA_BAND_BLKS)


def _bias_tile_plan(n_grid_rows):
    n_rb = n_grid_rows // NA_QROWS
    plans = []
    for rb in (0, 1, n_rb - 1):
        r0 = rb * NA_QROWS
        a = min(max(rb - 1, 0), n_rb - NA_BAND_BLKS) * NA_QROWS
        rows = []
        for j in range(NA_BAND_BLKS * NA_QROWS):
            kr = a + j
            pairs = []
            for ip in range(NA_QROWS // 2):
                drs = []
                for i in (2 * ip, 2 * ip + 1):
                    r = r0 + i
                    rs = min(max(r - NA_KH // 2, 0), n_grid_rows - NA_KH)
                    drs.append(kr - r + NA_KH - 1 if rs <= kr < rs + NA_KH else None)
                pairs.append(tuple(drs))
            rows.append(pairs)
        plans.append(rows)
    return plans


def _na_bias_kernel(rpb_ref, o_ref, *, plan):
    h = pl.program_id(0)
    n_dr = 2 * NA_KH - 1
    n_dc = 2 * NA_KW - 1
    shape = (GRID_W, 2 * GRID_W)
    kc = lax.broadcasted_iota(jnp.int32, shape, 0)
    lane = lax.broadcasted_iota(jnp.int32, shape, 1)
    qc = lane & (GRID_W - 1)
    right = lane >= GRID_W
    diff = kc - qc
    cstart = jnp.clip(qc - NA_KW // 2, 0, GRID_W - NA_KW)
    in_window = (kc >= cstart) & (kc < cstart + NA_KW)
    masked = jnp.full(shape, MASK_VALUE, F32)

    used = sorted({dr for rows in plan for pairs in rows for pr in pairs for dr in pr if dr is not None})
    tiles = {dr: masked for dr in used}
    for dc in range(n_dc):
        hit = in_window & (diff == dc - (NA_KW - 1))
        for dr in used:
            tiles[dr] = jnp.where(hit, rpb_ref[h * (n_dr * n_dc) + dr * n_dc + dc], tiles[dr])
    tiles[None] = masked

    for ty, rows in enumerate(plan):
        for j, pairs in enumerate(rows):
            for ip, (dl, dr) in enumerate(pairs):
                tile = tiles[dl] if dl == dr else jnp.where(right, tiles[dr], tiles[dl])
                o_ref[ty, 0, j * GRID_W:(j + 1) * GRID_W, ip * 2 * GRID_W:(ip + 1) * 2 * GRID_W] = tile


def _na_bias_table(rpb, n_grid_rows):
    heads = rpb.shape[0]
    band = NA_BAND_BLKS * NA_QBLK
    plan = _bias_tile_plan(n_grid_rows)
    return pl.pallas_call(
        functools.partial(_na_bias_kernel, plan=plan),
        grid=(heads,),
        in_specs=[pl.BlockSpec(memory_space=pltpu.SMEM)],
        out_specs=pl.BlockSpec((3, 1, band, NA_QBLK), lambda h: (0, h, 0, 0)),
        out_shape=jax.ShapeDtypeStruct((3, heads, band, NA_QBLK), F32),
        compiler_params=_cparams(1),
        name="na_bias_table",
    )(rpb.reshape(-1))


def _na_attn_kernel(q_ref, k0_ref, k1_ref, k2_ref, v0_ref, v1_ref, v2_ref,
                    ck_ref, cv_ref, bias_ref, o_ref):
    dh = NA_HEAD_DIM
    n_heads = q_ref.shape[1] // dh

    n_blk = NA_BAND_BLKS + 1
    k_refs = (k0_ref, k1_ref, k2_ref)
    v_refs = (v0_ref, v1_ref, v2_ref)

    def head_cols(hh):
        return slice(hh * dh, (hh + 1) * dh)

    def score_blk(hh, jb):
        sl = head_cols(hh)
        if jb < NA_BAND_BLKS:
            return (_dot_nt(k_refs[jb][:, sl], q_ref[:, sl])
                    + bias_ref[0, hh, jb * NA_QBLK:(jb + 1) * NA_QBLK, :])
        return _dot_nt(ck_ref[0, :, sl].astype(BF16), q_ref[:, sl])

    def value_blk(hh, jb):
        sl = head_cols(hh)
        return v_refs[jb][:, sl] if jb < NA_BAND_BLKS else cv_ref[0, :, sl].astype(BF16)

    def store(hh, o):
        o_ref[:, head_cols(hh)] = o.astype(BF16)

    _pipelined_heads(n_heads, n_blk, score_blk, value_blk, store)


def _na_attention(qkv, ck, cv, bias, batch, n_tok, heads_per_step=8):
    d = D_MODEL
    width = heads_per_step * NA_HEAD_DIM
    n_hg = NA_HEADS // heads_per_step
    n_rb = n_tok // NA_QBLK
    past = ck.shape[1]

    def qmap(hg, b, rb):
        return (b * n_rb + rb, hg)

    def band_map(col0, off):
        def f(hg, b, rb):
            return (b * n_rb + _band_start_blk(rb, n_rb) + off, col0 + hg)
        return f

    def bias_map(hg, b, rb):
        ty = jnp.where(rb == 0, 0, jnp.where(rb == n_rb - 1, 2, 1))
        return (ty, hg, 0, 0)

    blk = pl.BlockSpec((NA_QBLK, width), qmap)
    in_specs = [blk]
    in_specs += [pl.BlockSpec((NA_QBLK, width), band_map(n_hg, off)) for off in range(NA_BAND_BLKS)]
    in_specs += [pl.BlockSpec((NA_QBLK, width), band_map(2 * n_hg, off)) for off in range(NA_BAND_BLKS)]
    in_specs += [pl.BlockSpec((1, past, width), lambda hg, b, rb: (b, 0, hg))] * 2
    in_specs += [pl.BlockSpec((1, heads_per_step, NA_BAND_BLKS * NA_QBLK, NA_QBLK), bias_map)]
    return pl.pallas_call(
        _na_attn_kernel,
        grid=(n_hg, batch, n_rb),
        in_specs=in_specs,
        out_specs=blk,
        out_shape=jax.ShapeDtypeStruct((batch * n_tok, d), BF16),
        compiler_params=_cparams(3),
        name="na_attention",
    )(qkv, qkv, qkv, qkv, qkv, qkv, qkv, ck, cv, bias)


def _mix_out_kernel(o_ref, w_ref, x_ref, mod_ref, g_ref, x1_ref):
    ga = mod_ref[0, 2:3, :]
    chunks = _row_chunks(o_ref.shape[0])
    y_next = _dot(o_ref[chunks[0], :], w_ref[...])
    for idx, sl in enumerate(chunks):
        y = y_next
        if idx + 1 < len(chunks):
            y_next = _dot(o_ref[chunks[idx + 1], :], w_ref[...])
        x1_ref[sl, :] = x_ref[sl, :] + ga * _rms(y, g_ref[1:2, :])


def _mix_out(o, w, x, mod, g, tm=512):
    t, d = x.shape
    groups = mod.shape[0]
    tiles_per_group = t // groups // tm
    return pl.pallas_call(
        _mix_out_kernel,
        grid=(t // tm,),
        in_specs=[
            pl.BlockSpec((tm, d), lambda i: (i, 0)),
            pl.BlockSpec((d, d), lambda i: (0, 0)),
            pl.BlockSpec((tm, d), lambda i: (i, 0)),
            pl.BlockSpec((1, 6, d), lambda i: (i // tiles_per_group, 0, 0)),
            pl.BlockSpec((4, d), lambda i: (0, 0)),
        ],
        out_specs=pl.BlockSpec((tm, d), lambda i: (i, 0)),
        out_shape=jax.ShapeDtypeStruct((t, d), F32),
        compiler_params=_cparams(1),
        name="mix_out",
    )(o, w, x, mod, g)


def _ffn_kernel(x_ref, mod_ref, g_ref, wg_ref, wu_ref, wo_ref, x2_ref, h_scr):
    c = pl.program_id(1)
    last = pl.num_programs(1) - 1
    row_chunks = _row_chunks(x_ref.shape[0])

    def hidden(h):
        return (_silu(_dot(h, wg_ref[0])) * _dot(h, wu_ref[0])).astype(BF16)

    def over_chunks(get_h, emit):
        a_next = hidden(get_h(row_chunks[0]))
        w_down = wo_ref[0].astype(BF16)
        for idx, sl in enumerate(row_chunks):
            a = a_next
            if idx + 1 < len(row_chunks):
                a_next = hidden(get_h(row_chunks[idx + 1]))
            emit(sl, _dot(a, w_down))

    @pl.when(c == 0)
    def _():
        g, sh, sc = g_ref[2:3, :], mod_ref[0, 3:4, :], mod_ref[0, 4:5, :]

        def normed(sl):
            h = (_rms(x_ref[sl, :], g) * (1.0 + sc) + sh).astype(BF16)
            h_scr[sl, :] = h
            return h

        def assign(sl, part):
            x2_ref[sl, :] = part

        over_chunks(normed, assign)

    @pl.when((c > 0) & (c < last))
    def _():
        def accumulate(sl, part):
            x2_ref[sl, :] += part

        over_chunks(lambda sl: h_scr[sl, :], accumulate)

    @pl.when(c == last)
    def _():
        ga = mod_ref[0, 5:6, :]

        def finish(sl, part):
            y = x2_ref[sl, :] + part
            x2_ref[sl, :] = x_ref[sl, :] + ga * _rms(y, g_ref[3:4, :])

        over_chunks(lambda sl: h_scr[sl, :], finish)


def _ffn(x, mod, g, w_in, w_out, layer, tm=1024, tf=512):
    t, d = x.shape
    hidden = w_out.shape[1]
    n_c = hidden // tf
    groups = mod.shape[0]
    tiles_per_group = t // groups // tm
    row = pl.BlockSpec((tm, d), lambda i, c: (i, 0))
    return pl.pallas_call(
        _ffn_kernel,
        grid=(t // tm, n_c),
        in_specs=[
            row,
            pl.BlockSpec((1, 6, d), lambda i, c: (i // tiles_per_group, 0, 0)),
            pl.BlockSpec((4, d), lambda i, c: (0, 0)),
            pl.BlockSpec((1, d, tf), lambda i, c: (layer, 0, c)),
            pl.BlockSpec((1, d, tf), lambda i, c: (layer, 0, n_c + c)),
            pl.BlockSpec((1, tf, d), lambda i, c: (layer, c, 0)),
        ],
        out_specs=row,
        out_shape=jax.ShapeDtypeStruct((t, d), F32),
        scratch_shapes=[pltpu.VMEM((tm, d), BF16)],
        compiler_params=_cparams(2, FFN_VMEM_LIMIT_BYTES),
        name="ffn",
    )(x, mod, g, w_in, w_in, w_out)


def _ret_in_kernel(x_ref, mod_ref, g_ref, w_ref, *rest, rope, n_qk, n_q, n_v):
    if rope:
        cos_ref, sin_ref, o_ref, h_scr = rest
    else:
        o_ref, h_scr = rest
    j = pl.program_id(1)
    tn = w_ref.shape[1]
    half = RET_DK // 2
    chunks = _row_chunks(x_ref.shape[0])

    def store_qk(sl, acc, scale):
        for hh in range(tn // RET_DK):
            cols = slice(hh * RET_DK, (hh + 1) * RET_DK)
            x = acc[:, cols]
            if rope:
                rot = jnp.concatenate(
                    [pltpu.roll(x[:, :half], half // 2, 1), pltpu.roll(x[:, half:], half // 2, 1)], axis=1)
                x = x * cos_ref[sl, :] + rot * sin_ref[sl, :]
            o_ref[sl, cols] = (x * scale).astype(BF16)

    @pl.when(j == 0)
    def _():
        g, sc, sh = g_ref[0:1, :], mod_ref[0, 1:2, :], mod_ref[0, 0:1, :]
        for sl in chunks:
            h = (_rms(x_ref[sl, :], g) * (1.0 + sc) + sh).astype(BF16)
            h_scr[sl, :] = h
            store_qk(sl, _dot(h, w_ref[...]), RET_DK ** -0.5)

    @pl.when((j > 0) & (j < n_qk))
    def _():
        scale = jnp.where(j < n_q, RET_DK ** -0.5, 1.0).astype(F32)
        for sl in chunks:
            store_qk(sl, _dot(h_scr[sl, :], w_ref[...]), scale)

    @pl.when((j >= n_qk) & (j < n_qk + n_v))
    def _():
        for sl in chunks:
            o_ref[sl, :] = _dot(h_scr[sl, :], w_ref[...]).astype(BF16)

    @pl.when(j >= n_qk + n_v)
    def _():
        for sl in chunks:
            o_ref[sl, :] = _silu(_dot(h_scr[sl, :], w_ref[...])).astype(BF16)


def _ret_in_proj(x, mod, g, w, rope_tables=None, seq_len=None, tm=1024, tn=2048):
    t, d = x.shape
    n = w.shape[1]
    rope = rope_tables is not None
    groups = mod.shape[0]
    tiles_per_group = t // groups // tm
    in_specs = [
        pl.BlockSpec((tm, d), lambda i, j: (i, 0)),
        pl.BlockSpec((1, 6, d), lambda i, j: (i // tiles_per_group, 0, 0)),
        pl.BlockSpec((4, d), lambda i, j: (0, 0)),
        pl.BlockSpec((d, tn), lambda i, j: (0, j)),
    ]
    args = [x, mod, g, w]
    if rope:
        tiles_per_seq = seq_len // tm
        tab = pl.BlockSpec((tm, RET_DK), lambda i, j: (i % tiles_per_seq, 0))
        in_specs += [tab, tab]
        args += list(rope_tables)
    return pl.pallas_call(
        functools.partial(_ret_in_kernel, rope=rope, n_qk=2 * d // tn, n_q=d // tn, n_v=d // tn),
        grid=(t // tm, n // tn),
        in_specs=in_specs,
        out_specs=pl.BlockSpec((tm, tn), lambda i, j: (i, j)),
        out_shape=jax.ShapeDtypeStruct((t, n), BF16),
        scratch_shapes=[pltpu.VMEM((tm, d), BF16)],
        compiler_params=_cparams(2),
        name="ret_in_rope" if rope else "ret_in",
    )(*args)


def _rope_tables(n_tokens, dim):
    n_rows = n_tokens // GRID_W
    half = dim // 2
    quarter = dim // 4
    inv = jnp.power(ROPE_BASE, -jnp.arange(0, half, 2, dtype=F32) / half)
    ang_r = jnp.arange(n_rows, dtype=F32)[:, None] * inv[None, :]
    ang_c = jnp.arange(GRID_W, dtype=F32)[:, None] * inv[None, :]

    def table(fn, lo_sign):
        by_row = jnp.broadcast_to(fn(ang_r)[:, None, :], (n_rows, GRID_W, quarter))
        by_col = jnp.broadcast_to(fn(ang_c)[None, :, :], (n_rows, GRID_W, quarter))
        full = jnp.concatenate([lo_sign * by_row, by_row, lo_sign * by_col, by_col], axis=-1)
        return full.reshape(n_tokens, dim)

    return table(jnp.cos, 1.0), table(jnp.sin, -1.0)


def _log_sigmoid(x):
    y = -x
    return -(jnp.maximum(y, 0.0) + jnp.log1p(jnp.exp(-jnp.abs(y))))


def _group_norm(o):
    mu = jnp.mean(o, axis=-1, keepdims=True)
    dev = o - mu
    var = jnp.mean(dev * dev, axis=-1, keepdims=True)
    return dev * lax.rsqrt(var + EPS)


def _ret_kernel(dl_ref, q_ref, k_ref, v_ref, gf_ref, gb_ref, *rest,
                n_seq, seq_len, has_state, emit_state, unroll):
    rest = list(rest)
    if has_state:
        s0f_ref, s0b_ref = rest[:2]
        rest = rest[2:]
    y_ref = rest.pop(0)
    if emit_state:
        sf_out_ref, sb_out_ref = rest[:2]
        rest = rest[2:]
    of_scr, pb_scr, st_scr = rest

    head = pl.program_id(1)
    C = RET_CHUNK
    dk = RET_DK
    n_chunks = seq_len // C

    def lg(shape, direction):
        return _log_sigmoid(jnp.full(shape, dl_ref[direction, head], F32))

    ii = lax.broadcasted_iota(jnp.int32, (C, C), 0)
    jj = lax.broadcasted_iota(jnp.int32, (C, C), 1)
    causal = ii >= jj
    anti = jj >= ii
    dist = jnp.abs(ii - jj).astype(F32)
    decay_f = jnp.where(causal, jnp.exp(jnp.where(causal, dist, 0.0) * lg((C, C), 0)), 0.0)
    decay_b = jnp.where(anti, jnp.exp(jnp.where(anti, dist, 0.0) * lg((C, C), 1)), 0.0)
    pos = lax.broadcasted_iota(jnp.int32, (C, dk), 0).astype(F32)
    lgf = lg((C, dk), 0)
    lgb = lg((C, dk), 1)
    qd_f = jnp.exp((pos + 1.0) * lgf)
    kd_f = jnp.exp((C - 1.0 - pos) * lgf)
    qd_b = jnp.exp((C - pos) * lgb)
    kd_b = jnp.exp(pos * lgb)
    cd_f = jnp.exp(C * lg((1, dk), 0))
    cd_b = jnp.exp(C * lg((1, dk), 1))

    def chunk_rows(s, t):
        return pl.ds(pl.multiple_of(s * seq_len + t * C, C), C)

    def key_value(k, v, kd):
        return _dot_tn((k.astype(F32) * kd).astype(BF16), v)

    def fwd_local(s, t):
        rows = chunk_rows(s, t)
        q, k, v = q_ref[rows, :], k_ref[rows, :], v_ref[rows, :]
        raw = _dot_nt(q, k)
        pb_scr[rows, :] = (raw * decay_b).astype(BF16)
        return rows, q, _dot((raw * decay_f).astype(BF16), v), key_value(k, v, kd_f)

    def fwd_finish(local, state):
        rows, q, inner, kv = local
        of_scr[rows, :] = _group_norm(inner + _dot(q, state.astype(BF16)) * qd_f)
        return state * cd_f + kv

    def bwd_local(s, t):
        rows = chunk_rows(s, t)
        q, k, v = q_ref[rows, :], k_ref[rows, :], v_ref[rows, :]
        return rows, q, _dot(pb_scr[rows, :], v), key_value(k, v, kd_b)

    def bwd_finish(local, state):
        rows, q, inner, kv = local
        o = inner + _dot(q, state.astype(BF16)) * qd_b
        y = (gf_ref[rows, :].astype(F32) * of_scr[rows, :]
             + gb_ref[rows, :].astype(F32) * _group_norm(o))
        y_ref[rows, :] = y.astype(BF16)
        return state * cd_b + kv

    def sweep(local_fn, finish_fn, reverse):
        group = min(unroll, n_chunks)

        def run_group(gi):
            items = [(s, gi * group + c) for s in range(n_seq) for c in range(group)]
            states = {}

            def local_at(item):
                s, t = item
                return local_fn(s, n_chunks - 1 - t if reverse else t)

            nxt = local_at(items[0])
            for idx, (s, t) in enumerate(items):
                cur = nxt
                if idx + 1 < len(items):
                    nxt = local_at(items[idx + 1])
                state = states[s] if s in states else st_scr[s]
                states[s] = finish_fn(cur, state)
            for s in range(n_seq):
                st_scr[s] = states[s]

        n_groups = n_chunks // group
        if n_groups == 1:
            run_group(0)
        else:
            lax.fori_loop(0, n_groups, lambda gi, carry: (run_group(gi), carry)[1], 0)

    for s in range(n_seq):
        st_scr[s] = s0f_ref[s, 0] if has_state else jnp.zeros((dk, dk), F32)
    sweep(fwd_local, fwd_finish, reverse=False)
    for s in range(n_seq):
        if emit_state:
            sf_out_ref[s, 0, 0] = st_scr[s]
        st_scr[s] = s0b_ref[s, 0] if has_state else jnp.zeros((dk, dk), F32)
    sweep(bwd_local, bwd_finish, reverse=True)
    if emit_state:
        for s in range(n_seq):
            sb_out_ref[s, 0, 0] = st_scr[s]


def _retention(proj, decay_logit, seq_len, n_seq, state_f=None, state_b=None, emit_state=False):
    t = proj.shape[0]
    d = D_MODEL
    dk = RET_DK
    heads = RET_HEADS
    rows = n_seq * seq_len
    has_state = state_f is not None

    def col(offset):
        return pl.BlockSpec((rows, dk), lambda b, h: (b, offset * heads + h))

    in_specs = [pl.BlockSpec(memory_space=pltpu.SMEM)] + [col(o) for o in range(5)]
    args = [decay_logit, proj, proj, proj, proj, proj]
    if has_state:
        st = pl.BlockSpec((n_seq, 1, dk, dk), lambda b, h: (b, h, 0, 0))
        in_specs += [st, st]
        args += [state_f, state_b]
    out_specs = [pl.BlockSpec((rows, dk), lambda b, h: (b, h))]
    out_shape = [jax.ShapeDtypeStruct((t, d), BF16)]
    if emit_state:
        n_batch = t // seq_len
        st_out = pl.BlockSpec((n_seq, 1, 1, dk, dk), lambda b, h: (b, 0, h, 0, 0))
        out_specs += [st_out, st_out]
        out_shape += [jax.ShapeDtypeStruct((n_batch, 1, heads, dk, dk), F32)] * 2
    return pl.pallas_call(
        functools.partial(_ret_kernel, n_seq=n_seq, seq_len=seq_len,
                          has_state=has_state, emit_state=emit_state, unroll=16),
        grid=(t // rows, heads),
        in_specs=in_specs,
        out_specs=out_specs,
        out_shape=out_shape,
        scratch_shapes=[pltpu.VMEM((rows, dk), F32), pltpu.VMEM((rows, RET_CHUNK), BF16),
                        pltpu.VMEM((n_seq, dk, dk), F32)],
        compiler_params=_cparams(2),
        name="retention_state" if emit_state else "retention",
    )(*args)


def kernel(x_prompt, x_sample, cache_na_k, cache_na_v, state_ret_fwd, state_ret_bwd, c, c_ctx,
           w_ada, b_ada, g_norm, na_w_qkv, na_w_o, na_rpb, ret_w_in, ret_w_o, ret_decay_logit,
           ffn_w_in, ffn_w_out):
    bp, seq, d = x_prompt.shape
    bs, n_tok, _ = x_sample.shape
    past = cache_na_k.shape[2]
    xp = x_prompt.reshape(bp * seq, d)
    xs = x_sample.reshape(bs * n_tok, d)

    cond = jnp.concatenate([c_ctx[None, :], c, jnp.zeros((8 - 1 - bs, d), F32)], axis=0)
    mod = _ada_modulation(cond, w_ada, b_ada).reshape(w_ada.shape[0], 8, 6, d)
    mod_p = [mod[l, 0:1] for l in range(2)]
    mod_s = [mod[l, 1:1 + bs] for l in range(2)]

    w_qkv = na_w_qkv[0].astype(BF16)
    w_na_o = na_w_o[0].astype(BF16)
    w_ret_in = ret_w_in[0].astype(BF16)
    w_ret_o = ret_w_o[0].astype(BF16)
    w_ffn_in = ffn_w_in.astype(BF16)
    w_ffn_out = ffn_w_out

    qkv_p, k_p, v_p = _qkv_proj(xp, mod_p[0], g_norm[0], w_qkv, emit_kv=True)
    (qkv_s,) = _qkv_proj(xs, mod_s[0], g_norm[0], w_qkv, emit_kv=False)
    o_p = _ctx_attention(qkv_p, seq)
    bias = _na_bias_table(na_rpb[0], n_tok // GRID_W)
    ck = cache_na_k[:, 0].reshape(bs, past, d)
    cv = cache_na_v[:, 0].reshape(bs, past, d)
    o_s = _na_attention(qkv_s, ck, cv, bias, bs, n_tok)

    xp = _mix_out(o_p, w_na_o, xp, mod_p[0], g_norm[0])
    xs = _mix_out(o_s, w_na_o, xs, mod_s[0], g_norm[0])
    xp = _ffn(xp, mod_p[0], g_norm[0], w_ffn_in, w_ffn_out, 0)
    xs = _ffn(xs, mod_s[0], g_norm[0], w_ffn_in, w_ffn_out, 0)

    proj_p = _ret_in_proj(xp, mod_p[1], g_norm[1], w_ret_in)
    proj_s = _ret_in_proj(xs, mod_s[1], g_norm[1], w_ret_in, _rope_tables(n_tok, RET_DK), n_tok)
    y_p, sf, sb = _retention(proj_p, ret_decay_logit[0], seq, 8, emit_state=True)
    (y_s,) = _retention(proj_s, ret_decay_logit[0], n_tok, 1,
                        state_f=state_ret_fwd[:, 0], state_b=state_ret_bwd[:, 0])

    xp = _mix_out(y_p, w_ret_o, xp, mod_p[1], g_norm[1])
    xs = _mix_out(y_s, w_ret_o, xs, mod_s[1], g_norm[1])
    xp = _ffn(xp, mod_p[1], g_norm[1], w_ffn_in, w_ffn_out, 1)
    xs = _ffn(xs, mod_s[1], g_norm[1], w_ffn_in, w_ffn_out, 1)

    kv_shape = (bp, 1, seq, NA_HEADS, NA_HEAD_DIM)
    return (xp.reshape(bp, seq, d), xs.reshape(bs, n_tok, d),
            k_p.reshape(kv_shape), v_p.reshape(kv_shape),
            sf.astype(x_prompt.dtype), sb.astype(x_prompt.dtype))
```

```python
import functools

import jax
import jax.numpy as jnp
from jax import lax
from jax.experimental import pallas as pl
from jax.experimental.pallas import tpu as pltpu

D_MODEL = 2048
GRID_W = 64
NA_HEADS = 16
NA_HEAD_DIM = D_MODEL // NA_HEADS
NA_KH = 8
NA_KW = 16
RET_HEADS = 8
RET_DK = D_MODEL // RET_HEADS
RET_CHUNK = 128
FFN_HIDDEN = -(-8 * D_MODEL // (3 * 256)) * 256
ROPE_BASE = 10000.0
EPS = 1e-6

F32 = jnp.float32
BF16 = jnp.bfloat16
MASK_VALUE = -1e30

VMEM_LIMIT_BYTES = 56 * 1024 * 1024
FFN_VMEM_LIMIT_BYTES = 60 * 1024 * 1024
ROW_CHUNK = 256
NA_QROWS = 4
NA_QBLK = NA_QROWS * GRID_W
NA_BAND_BLKS = 3


def _cparams(n_axes, vmem_limit_bytes=VMEM_LIMIT_BYTES):
    return pltpu.CompilerParams(
        dimension_semantics=("arbitrary",) * n_axes,
        vmem_limit_bytes=vmem_limit_bytes)


def _silu(x):
    return x / (1.0 + jnp.exp(-x))


def _rms(x, g):
    ms = jnp.mean(x * x, axis=-1, keepdims=True)
    return x * lax.rsqrt(ms + EPS) * g


def _dot(a, b):
    return jnp.dot(a, b, preferred_element_type=F32)


def _dot_nt(a, b):
    return lax.dot_general(a, b, (((1,), (1,)), ((), ())), preferred_element_type=F32)


def _dot_tn(a, b):
    return lax.dot_general(a, b, (((0,), (0,)), ((), ())), preferred_element_type=F32)


def _ada_kernel(cond_ref, w_ref, b_ref, o_ref):
    a = _silu(cond_ref[...]).astype(BF16)
    o_ref[0] = _dot(a, w_ref[0].astype(BF16)) + b_ref[0]


def _ada_modulation(cond, w_ada, b_ada, tn=1024):
    depth, d, n = w_ada.shape
    rows = cond.shape[0]
    return pl.pallas_call(
        _ada_kernel,
        grid=(depth, n // tn),
        in_specs=[
            pl.BlockSpec((rows, d), lambda l, j: (0, 0)),
            pl.BlockSpec((1, d, tn), lambda l, j: (l, 0, j)),
            pl.BlockSpec((1, 1, tn), lambda l, j: (l, 0, j)),
        ],
        out_specs=pl.BlockSpec((1, rows, tn), lambda l, j: (l, 0, j)),
        out_shape=jax.ShapeDtypeStruct((depth, rows, n), F32),
        compiler_params=_cparams(2),
        name="ada_modulation",
    )(cond, w_ada, b_ada.reshape(depth, 1, n))


def _row_chunks(n_rows):
    return [slice(r * ROW_CHUNK, (r + 1) * ROW_CHUNK) for r in range(n_rows // ROW_CHUNK)]


def _qkv_kernel(x_ref, mod_ref, g_ref, w_ref, *rest, nq, nk, emit_kv):
    if emit_kv:
        o_ref, k_ref, v_ref, h_scr = rest
    else:
        o_ref, h_scr = rest
    j = pl.program_id(1)
    chunks = _row_chunks(x_ref.shape[0])
    q_scale = NA_HEAD_DIM ** -0.5

    @pl.when(j == 0)
    def _():
        g, sc, sh = g_ref[0:1, :], mod_ref[0, 1:2, :], mod_ref[0, 0:1, :]
        for sl in chunks:
            h = (_rms(x_ref[sl, :], g) * (1.0 + sc) + sh).astype(BF16)
            h_scr[sl, :] = h
            o_ref[sl, :] = (_dot(h, w_ref[...]) * q_scale).astype(BF16)

    @pl.when((j > 0) & (j < nq))
    def _():
        for sl in chunks:
            o_ref[sl, :] = (_dot(h_scr[sl, :], w_ref[...]) * q_scale).astype(BF16)

    @pl.when((j >= nq) & (j < nq + nk))
    def _():
        for sl in chunks:
            acc = _dot(h_scr[sl, :], w_ref[...])
            o_ref[sl, :] = acc.astype(BF16)
            if emit_kv:
                k_ref[sl, :] = acc

    @pl.when(j >= nq + nk)
    def _():
        for sl in chunks:
            acc = _dot(h_scr[sl, :], w_ref[...])
            o_ref[sl, :] = acc.astype(BF16)
            if emit_kv:
                v_ref[sl, :] = acc


def _qkv_proj(x, mod, g, w, *, emit_kv):
    t, d = x.shape
    tm = 1024
    tn = 1024 if emit_kv else 2048
    n = w.shape[1]
    groups = mod.shape[0]
    tiles_per_group = t // groups // tm
    nq = d // tn
    nk = d // tn
    out_shape = [jax.ShapeDtypeStruct((t, n), BF16)]
    out_specs = [pl.BlockSpec((tm, tn), lambda i, j: (i, j))]
    if emit_kv:
        out_shape += [jax.ShapeDtypeStruct((t, d), F32)] * 2
        out_specs += [
            pl.BlockSpec((tm, tn), lambda i, j: (i, jnp.clip(j - nq, 0, nk - 1))),
            pl.BlockSpec((tm, tn), lambda i, j: (i, jnp.clip(j - nq - nk, 0, nk - 1))),
        ]
    return pl.pallas_call(
        functools.partial(_qkv_kernel, nq=nq, nk=nk, emit_kv=emit_kv),
        grid=(t // tm, n // tn),
        in_specs=[
            pl.BlockSpec((tm, d), lambda i, j: (i, 0)),
            pl.BlockSpec((1, 6, d), lambda i, j: (i // tiles_per_group, 0, 0)),
            pl.BlockSpec((4, d), lambda i, j: (0, 0)),
            pl.BlockSpec((d, tn), lambda i, j: (0, j)),
        ],
        out_specs=out_specs,
        out_shape=out_shape,
        scratch_shapes=[pltpu.VMEM((tm, d), BF16)],
        compiler_params=_cparams(2),
        name="qkv_proj_kv" if emit_kv else "qkv_proj",
    )(x, mod, g, w)


def _pipelined_heads(n_heads, n_blk, score_blk, value_blk, store):
    s_new = p_new = l_new = None
    for step in range(n_heads + 2):
        h_s, h_e, h_o = step, step - 1, step - 2
        s_cur, s_new = s_new, []
        p_cur, p_new = p_new, []
        l_cur, l_new = l_new, None
        exp_on = 0 <= h_e < n_heads
        out_on = 0 <= h_o < n_heads
        if exp_on:
            m = s_cur[0].max(axis=0, keepdims=True)
            for sj in s_cur[1:]:
                m = jnp.maximum(m, sj.max(axis=0, keepdims=True))
        o_t = None
        for jb in range(n_blk):
            if h_s < n_heads:
                s_new.append(score_blk(h_s, jb))
            if exp_on:
                p = jnp.exp(s_cur[jb] - m)
                part = p.sum(axis=0, keepdims=True)
                l_new = part if l_new is None else l_new + part
                p_new.append(p.astype(BF16))
            if out_on:
                part = _dot_tn(value_blk(h_o, jb), p_cur[jb])
                o_t = part if o_t is None else o_t + part
        if out_on:
            store(h_o, (o_t * (1.0 / l_cur)).T)


def _ctx_attn_kernel(q_ref, k_ref, v_ref, o_ref):
    def cols(h):
        return slice(h * NA_HEAD_DIM, (h + 1) * NA_HEAD_DIM)

    def scores(h):
        return _dot_nt(q_ref[:, cols(h)], k_ref[:, cols(h)])

    s_next = scores(0)
    for h in range(NA_HEADS):
        s = s_next
        if h + 1 < NA_HEADS:
            s_next = scores(h + 1)
        m = jnp.max(s, axis=-1, keepdims=True)
        p = jnp.exp(s - m)
        l = jnp.sum(p, axis=-1, keepdims=True)
        o = _dot(p.astype(BF16), v_ref[:, cols(h)])
        o_ref[:, cols(h)] = (o * (1.0 / l)).astype(BF16)


def _ctx_attention(qkv, seq):
    t = qkv.shape[0]
    d = D_MODEL
    return pl.pallas_call(
        _ctx_attn_kernel,
        grid=(t // seq,),
        in_specs=[
            pl.BlockSpec((seq, d), lambda b: (b, 0)),
            pl.BlockSpec((seq, d), lambda b: (b, 1)),
            pl.BlockSpec((seq, d), lambda b: (b, 2)),
        ],
        out_specs=pl.BlockSpec((seq, d), lambda b: (b, 0)),
        out_shape=jax.ShapeDtypeStruct((t, d), BF16),
        compiler_params=_cparams(1),
        name="ctx_attention",
    )(qkv, qkv, qkv)


def _band_start_blk(rb, n_rb):
    return jnp.clip(rb - 1, 0, n_rb - NA_BAND_BLKS)


def _bias_tile_plan(n_grid_rows):
    n_rb = n_grid_rows // NA_QROWS
    plans = []
    for rb in (0, 1, n_rb - 1):
        r0 = rb * NA_QROWS
        a = min(max(rb - 1, 0), n_rb - NA_BAND_BLKS) * NA_QROWS
        rows = []
        for j in range(NA_BAND_BLKS * NA_QROWS):
            kr = a + j
            pairs = []
            for ip in range(NA_QROWS // 2):
                drs = []
                for i in (2 * ip, 2 * ip + 1):
                    r = r0 + i
                    rs = min(max(r - NA_KH // 2, 0), n_grid_rows - NA_KH)
                    drs.append(kr - r + NA_KH - 1 if rs <= kr < rs + NA_KH else None)
                pairs.append(tuple(drs))
            rows.append(pairs)
        plans.append(rows)
    return plans


def _na_bias_kernel(rpb_ref, o_ref, *, plan):
    h = pl.program_id(0)
    n_dr = 2 * NA_KH - 1
    n_dc = 2 * NA_KW - 1
    shape = (GRID_W, 2 * GRID_W)
    kc = lax.broadcasted_iota(jnp.int32, shape, 0)
    lane = lax.broadcasted_iota(jnp.int32, shape, 1)
    qc = lane & (GRID_W - 1)
    right = lane >= GRID_W
    diff = kc - qc
    cstart = jnp.clip(qc - NA_KW // 2, 0, GRID_W - NA_KW)
    in_window = (kc >= cstart) & (kc < cstart + NA_KW)
    masked = jnp.full(shape, MASK_VALUE, F32)

    used = sorted({dr for rows in plan for pairs in rows for pr in pairs for dr in pr if dr is not None})
    tiles = {dr: masked for dr in used}
    for dc in range(n_dc):
        hit = in_window & (diff == dc - (NA_KW - 1))
        for dr in used:
            tiles[dr] = jnp.where(hit, rpb_ref[h * (n_dr * n_dc) + dr * n_dc + dc], tiles[dr])
    tiles[None] = masked

    for ty, rows in enumerate(plan):
        for j, pairs in enumerate(rows):
            for ip, (dl, dr) in enumerate(pairs):
                tile = tiles[dl] if dl == dr else jnp.where(right, tiles[dr], tiles[dl])
                o_ref[ty, 0, j * GRID_W:(j + 1) * GRID_W, ip * 2 * GRID_W:(ip + 1) * 2 * GRID_W] = tile


def _na_bias_table(rpb, n_grid_rows):
    heads = rpb.shape[0]
    band = NA_BAND_BLKS * NA_QBLK
    plan = _bias_tile_plan(n_grid_rows)
    return pl.pallas_call(
        functools.partial(_na_bias_kernel, plan=plan),
        grid=(heads,),
        in_specs=[pl.BlockSpec(memory_space=pltpu.SMEM)],
        out_specs=pl.BlockSpec((3, 1, band, NA_QBLK), lambda h: (0, h, 0, 0)),
        out_shape=jax.ShapeDtypeStruct((3, heads, band, NA_QBLK), F32),
        compiler_params=_cparams(1),
        name="na_bias_table",
    )(rpb.reshape(-1))


def _na_attn_kernel(q_ref, k0_ref, k1_ref, k2_ref, v0_ref, v1_ref, v2_ref,
                    ck_ref, cv_ref, bias_ref, o_ref):
    dh = NA_HEAD_DIM
    n_heads = q_ref.shape[1] // dh

    n_blk = NA_BAND_BLKS + 1
    k_refs = (k0_ref, k1_ref, k2_ref)
    v_refs = (v0_ref, v1_ref, v2_ref)

    def head_cols(hh):
        return slice(hh * dh, (hh + 1) * dh)

    def score_blk(hh, jb):
        sl = head_cols(hh)
        if jb < NA_BAND_BLKS:
            return (_dot_nt(k_refs[jb][:, sl], q_ref[:, sl])
                    + bias_ref[0, hh, jb * NA_QBLK:(jb + 1) * NA_QBLK, :])
        return _dot_nt(ck_ref[0, :, sl].astype(BF16), q_ref[:, sl])

    def value_blk(hh, jb):
        sl = head_cols(hh)
        return v_refs[jb][:, sl] if jb < NA_BAND_BLKS else cv_ref[0, :, sl].astype(BF16)

    def store(hh, o):
        o_ref[:, head_cols(hh)] = o.astype(BF16)

    _pipelined_heads(n_heads, n_blk, score_blk, value_blk, store)


def _na_attention(qkv, ck, cv, bias, batch, n_tok, heads_per_step=16):
    d = D_MODEL
    width = heads_per_step * NA_HEAD_DIM
    n_hg = NA_HEADS // heads_per_step
    n_rb = n_tok // NA_QBLK
    past = ck.shape[1]

    def qmap(hg, b, rb):
        return (b * n_rb + rb, hg)

    def band_map(col0, off):
        def f(hg, b, rb):
            return (b * n_rb + _band_start_blk(rb, n_rb) + off, col0 + hg)
        return f

    def bias_map(hg, b, rb):
        ty = jnp.where(rb == 0, 0, jnp.where(rb == n_rb - 1, 2, 1))
        return (ty, hg, 0, 0)

    blk = pl.BlockSpec((NA_QBLK, width), qmap)
    in_specs = [blk]
    in_specs += [pl.BlockSpec((NA_QBLK, width), band_map(n_hg, off)) for off in range(NA_BAND_BLKS)]
    in_specs += [pl.BlockSpec((NA_QBLK, width), band_map(2 * n_hg, off)) for off in range(NA_BAND_BLKS)]
    in_specs += [pl.BlockSpec((1, past, width), lambda hg, b, rb: (b, 0, hg))] * 2
    in_specs += [pl.BlockSpec((1, heads_per_step, NA_BAND_BLKS * NA_QBLK, NA_QBLK), bias_map)]
    return pl.pallas_call(
        _na_attn_kernel,
        grid=(n_hg, batch, n_rb),
        in_specs=in_specs,
        out_specs=blk,
        out_shape=jax.ShapeDtypeStruct((batch * n_tok, d), BF16),
        compiler_params=_cparams(3),
        name="na_attention",
    )(qkv, qkv, qkv, qkv, qkv, qkv, qkv, ck, cv, bias)


def _mix_out_kernel(o_ref, w_ref, x_ref, mod_ref, g_ref, x1_ref):
    ga = mod_ref[0, 2:3, :]
    chunks = _row_chunks(o_ref.shape[0])
    y_next = _dot(o_ref[chunks[0], :], w_ref[...])
    for idx, sl in enumerate(chunks):
        y = y_next
        if idx + 1 < len(chunks):
            y_next = _dot(o_ref[chunks[idx + 1], :], w_ref[...])
        x1_ref[sl, :] = x_ref[sl, :] + ga * _rms(y, g_ref[1:2, :])


def _mix_out(o, w, x, mod, g, tm=512):
    t, d = x.shape
    groups = mod.shape[0]
    tiles_per_group = t // groups // tm
    return pl.pallas_call(
        _mix_out_kernel,
        grid=(t // tm,),
        in_specs=[
            pl.BlockSpec((tm, d), lambda i: (i, 0)),
            pl.BlockSpec((d, d), lambda i: (0, 0)),
            pl.BlockSpec((tm, d), lambda i: (i, 0)),
            pl.BlockSpec((1, 6, d), lambda i: (i // tiles_per_group, 0, 0)),
            pl.BlockSpec((4, d), lambda i: (0, 0)),
        ],
        out_specs=pl.BlockSpec((tm, d), lambda i: (i, 0)),
        out_shape=jax.ShapeDtypeStruct((t, d), F32),
        compiler_params=_cparams(1),
        name="mix_out",
    )(o, w, x, mod, g)


def _ffn_kernel(x_ref, mod_ref, g_ref, wg_ref, wu_ref, wo_ref, x2_ref, h_scr):
    c = pl.program_id(1)
    last = pl.num_programs(1) - 1
    row_chunks = _row_chunks(x_ref.shape[0])

    def hidden(h):
        return (_silu(_dot(h, wg_ref[0])) * _dot(h, wu_ref[0])).astype(BF16)

    def over_chunks(get_h, emit):
        a_next = hidden(get_h(row_chunks[0]))
        w_down = wo_ref[0].astype(BF16)
        for idx, sl in enumerate(row_chunks):
            a = a_next
            if idx + 1 < len(row_chunks):
                a_next = hidden(get_h(row_chunks[idx + 1]))
            emit(sl, _dot(a, w_down))

    @pl.when(c == 0)
    def _():
        g, sh, sc = g_ref[2:3, :], mod_ref[0, 3:4, :], mod_ref[0, 4:5, :]

        def normed(sl):
            h = (_rms(x_ref[sl, :], g) * (1.0 + sc) + sh).astype(BF16)
            h_scr[sl, :] = h
            return h

        def assign(sl, part):
            x2_ref[sl, :] = part

        over_chunks(normed, assign)

    @pl.when((c > 0) & (c < last))
    def _():
        def accumulate(sl, part):
            x2_ref[sl, :] += part

        over_chunks(lambda sl: h_scr[sl, :], accumulate)

    @pl.when(c == last)
    def _():
        ga = mod_ref[0, 5:6, :]

        def finish(sl, part):
            y = x2_ref[sl, :] + part
            x2_ref[sl, :] = x_ref[sl, :] + ga * _rms(y, g_ref[3:4, :])

        over_chunks(lambda sl: h_scr[sl, :], finish)


def _ffn(x, mod, g, w_in, w_out, layer, tm=1024, tf=512):
    t, d = x.shape
    hidden = w_out.shape[1]
    n_c = hidden // tf
    groups = mod.shape[0]
    tiles_per_group = t // groups // tm
    row = pl.BlockSpec((tm, d), lambda i, c: (i, 0))
    return pl.pallas_call(
        _ffn_kernel,
        grid=(t // tm, n_c),
        in_specs=[
            row,
            pl.BlockSpec((1, 6, d), lambda i, c: (i // tiles_per_group, 0, 0)),
            pl.BlockSpec((4, d), lambda i, c: (0, 0)),
            pl.BlockSpec((1, d, tf), lambda i, c: (layer, 0, c)),
            pl.BlockSpec((1, d, tf), lambda i, c: (layer, 0, n_c + c)),
            pl.BlockSpec((1, tf, d), lambda i, c: (layer, c, 0)),
        ],
        out_specs=row,
        out_shape=jax.ShapeDtypeStruct((t, d), F32),
        scratch_shapes=[pltpu.VMEM((tm, d), BF16)],
        compiler_params=_cparams(2, FFN_VMEM_LIMIT_BYTES),
        name="ffn",
    )(x, mod, g, w_in, w_in, w_out)


def _ret_in_kernel(x_ref, mod_ref, g_ref, w_ref, *rest, rope, n_qk, n_q, n_v):
    if rope:
        cos_ref, sin_ref, o_ref, h_scr = rest
    else:
        o_ref, h_scr = rest
    j = pl.program_id(1)
    tn = w_ref.shape[1]
    half = RET_DK // 2
    chunks = _row_chunks(x_ref.shape[0])

    def store_qk(sl, acc, scale):
        for hh in range(tn // RET_DK):
            cols = slice(hh * RET_DK, (hh + 1) * RET_DK)
            x = acc[:, cols]
            if rope:
                rot = jnp.concatenate(
                    [pltpu.roll(x[:, :half], half // 2, 1), pltpu.roll(x[:, half:], half // 2, 1)], axis=1)
                x = x * cos_ref[sl, :] + rot * sin_ref[sl, :]
            o_ref[sl, cols] = (x * scale).astype(BF16)

    @pl.when(j == 0)
    def _():
        g, sc, sh = g_ref[0:1, :], mod_ref[0, 1:2, :], mod_ref[0, 0:1, :]
        for sl in chunks:
            h = (_rms(x_ref[sl, :], g) * (1.0 + sc) + sh).astype(BF16)
            h_scr[sl, :] = h
            store_qk(sl, _dot(h, w_ref[...]), RET_DK ** -0.5)

    @pl.when((j > 0) & (j < n_qk))
    def _():
        scale = jnp.where(j < n_q, RET_DK ** -0.5, 1.0).astype(F32)
        for sl in chunks:
            store_qk(sl, _dot(h_scr[sl, :], w_ref[...]), scale)

    @pl.when((j >= n_qk) & (j < n_qk + n_v))
    def _():
        for sl in chunks:
            o_ref[sl, :] = _dot(h_scr[sl, :], w_ref[...]).astype(BF16)

    @pl.when(j >= n_qk + n_v)
    def _():
        for sl in chunks:
            o_ref[sl, :] = _silu(_dot(h_scr[sl, :], w_ref[...])).astype(BF16)


def _ret_in_proj(x, mod, g, w, rope_tables=None, seq_len=None, tm=1024, tn=2048):
    t, d = x.shape
    n = w.shape[1]
    rope = rope_tables is not None
    groups = mod.shape[0]
    tiles_per_group = t // groups // tm
    in_specs = [
        pl.BlockSpec((tm, d), lambda i, j: (i, 0)),
        pl.BlockSpec((1, 6, d), lambda i, j: (i // tiles_per_group, 0, 0)),
        pl.BlockSpec((4, d), lambda i, j: (0, 0)),
        pl.BlockSpec((d, tn), lambda i, j: (0, j)),
    ]
    args = [x, mod, g, w]
    if rope:
        tiles_per_seq = seq_len // tm
        tab = pl.BlockSpec((tm, RET_DK), lambda i, j: (i % tiles_per_seq, 0))
        in_specs += [tab, tab]
        args += list(rope_tables)
    return pl.pallas_call(
        functools.partial(_ret_in_kernel, rope=rope, n_qk=2 * d // tn, n_q=d // tn, n_v=d // tn),
        grid=(t // tm, n // tn),
        in_specs=in_specs,
        out_specs=pl.BlockSpec((tm, tn), lambda i, j: (i, j)),
        out_shape=jax.ShapeDtypeStruct((t, n), BF16),
        scratch_shapes=[pltpu.VMEM((tm, d), BF16)],
        compiler_params=_cparams(2),
        name="ret_in_rope" if rope else "ret_in",
    )(*args)


def _rope_tables(n_tokens, dim):
    n_rows = n_tokens // GRID_W
    half = dim // 2
    quarter = dim // 4
    inv = jnp.power(ROPE_BASE, -jnp.arange(0, half, 2, dtype=F32) / half)
    ang_r = jnp.arange(n_rows, dtype=F32)[:, None] * inv[None, :]
    ang_c = jnp.arange(GRID_W, dtype=F32)[:, None] * inv[None, :]

    def table(fn, lo_sign):
        by_row = jnp.broadcast_to(fn(ang_r)[:, None, :], (n_rows, GRID_W, quarter))
        by_col = jnp.broadcast_to(fn(ang_c)[None, :, :], (n_rows, GRID_W, quarter))
        full = jnp.concatenate([lo_sign * by_row, by_row, lo_sign * by_col, by_col], axis=-1)
        return full.reshape(n_tokens, dim)

    return table(jnp.cos, 1.0), table(jnp.sin, -1.0)


def _log_sigmoid(x):
    y = -x
    return -(jnp.maximum(y, 0.0) + jnp.log1p(jnp.exp(-jnp.abs(y))))


def _group_norm(o):
    mu = jnp.mean(o, axis=-1, keepdims=True)
    dev = o - mu
    var = jnp.mean(dev * dev, axis=-1, keepdims=True)
    return dev * lax.rsqrt(var + EPS)


def _ret_kernel(dl_ref, q_ref, k_ref, v_ref, gf_ref, gb_ref, *rest,
                n_seq, seq_len, has_state, emit_state, unroll):
    rest = list(rest)
    if has_state:
        s0f_ref, s0b_ref = rest[:2]
        rest = rest[2:]
    y_ref = rest.pop(0)
    if emit_state:
        sf_out_ref, sb_out_ref = rest[:2]
        rest = rest[2:]
    of_scr, pb_scr, st_scr = rest

    head = pl.program_id(1)
    C = RET_CHUNK
    dk = RET_DK
    n_chunks = seq_len // C

    def lg(shape, direction):
        return _log_sigmoid(jnp.full(shape, dl_ref[direction, head], F32))

    ii = lax.broadcasted_iota(jnp.int32, (C, C), 0)
    jj = lax.broadcasted_iota(jnp.int32, (C, C), 1)
    causal = ii >= jj
    anti = jj >= ii
    dist = jnp.abs(ii - jj).astype(F32)
    decay_f = jnp.where(causal, jnp.exp(jnp.where(causal, dist, 0.0) * lg((C, C), 0)), 0.0)
    decay_b = jnp.where(anti, jnp.exp(jnp.where(anti, dist, 0.0) * lg((C, C), 1)), 0.0)
    pos = lax.broadcasted_iota(jnp.int32, (C, dk), 0).astype(F32)
    lgf = lg((C, dk), 0)
    lgb = lg((C, dk), 1)
    qd_f = jnp.exp((pos + 1.0) * lgf)
    kd_f = jnp.exp((C - 1.0 - pos) * lgf)
    qd_b = jnp.exp((C - pos) * lgb)
    kd_b = jnp.exp(pos * lgb)
    cd_f = jnp.exp(C * lg((1, dk), 0))
    cd_b = jnp.exp(C * lg((1, dk), 1))

    def chunk_rows(s, t):
        return pl.ds(pl.multiple_of(s * seq_len + t * C, C), C)

    def key_value(k, v, kd):
        return _dot_tn((k.astype(F32) * kd).astype(BF16), v)

    def fwd_local(s, t):
        rows = chunk_rows(s, t)
        q, k, v = q_ref[rows, :], k_ref[rows, :], v_ref[rows, :]
        raw = _dot_nt(q, k)
        pb_scr[rows, :] = (raw * decay_b).astype(BF16)
        return rows, q, _dot((raw * decay_f).astype(BF16), v), key_value(k, v, kd_f)

    def fwd_finish(local, state):
        rows, q, inner, kv = local
        of_scr[rows, :] = _group_norm(inner + _dot(q, state.astype(BF16)) * qd_f)
        return state * cd_f + kv

    def bwd_local(s, t):
        rows = chunk_rows(s, t)
        q, k, v = q_ref[rows, :], k_ref[rows, :], v_ref[rows, :]
        return rows, q, _dot(pb_scr[rows, :], v), key_value(k, v, kd_b)

    def bwd_finish(local, state):
        rows, q, inner, kv = local
        o = inner + _dot(q, state.astype(BF16)) * qd_b
        y = (gf_ref[rows, :].astype(F32) * of_scr[rows, :]
             + gb_ref[rows, :].astype(F32) * _group_norm(o))
        y_ref[rows, :] = y.astype(BF16)
        return state * cd_b + kv

    def sweep(local_fn, finish_fn, reverse):
        group = min(unroll, n_chunks)

        def run_group(gi):
            items = [(s, gi * group + c) for s in range(n_seq) for c in range(group)]
            states = {}

            def local_at(item):
                s, t = item
                return local_fn(s, n_chunks - 1 - t if reverse else t)

            nxt = local_at(items[0])
            for idx, (s, t) in enumerate(items):
                cur = nxt
                if idx + 1 < len(items):
                    nxt = local_at(items[idx + 1])
                state = states[s] if s in states else st_scr[s]
                states[s] = finish_fn(cur, state)
            for s in range(n_seq):
                st_scr[s] = states[s]

        n_groups = n_chunks // group
        if n_groups == 1:
            run_group(0)
        else:
            lax.fori_loop(0, n_groups, lambda gi, carry: (run_group(gi), carry)[1], 0)

    for s in range(n_seq):
        st_scr[s] = s0f_ref[s, 0] if has_state else jnp.zeros((dk, dk), F32)
    sweep(fwd_local, fwd_finish, reverse=False)
    for s in range(n_seq):
        if emit_state:
            sf_out_ref[s, 0, 0] = st_scr[s]
        st_scr[s] = s0b_ref[s, 0] if has_state else jnp.zeros((dk, dk), F32)
    sweep(bwd_local, bwd_finish, reverse=True)
    if emit_state:
        for s in range(n_seq):
            sb_out_ref[s, 0, 0] = st_scr[s]


def _retention(proj, decay_logit, seq_len, n_seq, state_f=None, state_b=None, emit_state=False):
    t = proj.shape[0]
    d = D_MODEL
    dk = RET_DK
    heads = RET_HEADS
    rows = n_seq * seq_len
    has_state = state_f is not None

    def col(offset):
        return pl.BlockSpec((rows, dk), lambda b, h: (b, offset * heads + h))

    in_specs = [pl.BlockSpec(memory_space=pltpu.SMEM)] + [col(o) for o in range(5)]
    args = [decay_logit, proj, proj, proj, proj, proj]
    if has_state:
        st = pl.BlockSpec((n_seq, 1, dk, dk), lambda b, h: (b, h, 0, 0))
        in_specs += [st, st]
        args += [state_f, state_b]
    out_specs = [pl.BlockSpec((rows, dk), lambda b, h: (b, h))]
    out_shape = [jax.ShapeDtypeStruct((t, d), BF16)]
    if emit_state:
        n_batch = t // seq_len
        st_out = pl.BlockSpec((n_seq, 1, 1, dk, dk), lambda b, h: (b, 0, h, 0, 0))
        out_specs += [st_out, st_out]
        out_shape += [jax.ShapeDtypeStruct((n_batch, 1, heads, dk, dk), F32)] * 2
    return pl.pallas_call(
        functools.partial(_ret_kernel, n_seq=n_seq, seq_len=seq_len,
                          has_state=has_state, emit_state=emit_state, unroll=32),
        grid=(t // rows, heads),
        in_specs=in_specs,
        out_specs=out_specs,
        out_shape=out_shape,
        scratch_shapes=[pltpu.VMEM((rows, dk), F32), pltpu.VMEM((rows, RET_CHUNK), BF16),
                        pltpu.VMEM((n_seq, dk, dk), F32)],
        compiler_params=_cparams(2),
        name="retention_state" if emit_state else "retention",
    )(*args)


def kernel(x_prompt, x_sample, cache_na_k, cache_na_v, state_ret_fwd, state_ret_bwd, c, c_ctx,
           w_ada, b_ada, g_norm, na_w_qkv, na_w_o, na_rpb, ret_w_in, ret_w_o, ret_decay_logit,
           ffn_w_in, ffn_w_out):
    bp, seq, d = x_prompt.shape
    bs, n_tok, _ = x_sample.shape
    past = cache_na_k.shape[2]
    xp = x_prompt.reshape(bp * seq, d)
    xs = x_sample.reshape(bs * n_tok, d)

    cond = jnp.concatenate([c_ctx[None, :], c, jnp.zeros((8 - 1 - bs, d), F32)], axis=0)
    mod = _ada_modulation(cond, w_ada, b_ada).reshape(w_ada.shape[0], 8, 6, d)
    mod_p = [mod[l, 0:1] for l in range(2)]
    mod_s = [mod[l, 1:1 + bs] for l in range(2)]

    w_qkv = na_w_qkv[0].astype(BF16)
    w_na_o = na_w_o[0].astype(BF16)
    w_ret_in = ret_w_in[0].astype(BF16)
    w_ret_o = ret_w_o[0].astype(BF16)
    w_ffn_in = ffn_w_in.astype(BF16)
    w_ffn_out = ffn_w_out

    qkv_p, k_p, v_p = _qkv_proj(xp, mod_p[0], g_norm[0], w_qkv, emit_kv=True)
    (qkv_s,) = _qkv_proj(xs, mod_s[0], g_norm[0], w_qkv, emit_kv=False)
    o_p = _ctx_attention(qkv_p, seq)
    bias = _na_bias_table(na_rpb[0], n_tok // GRID_W)
    ck = cache_na_k[:, 0].reshape(bs, past, d)
    cv = cache_na_v[:, 0].reshape(bs, past, d)
    o_s = _na_attention(qkv_s, ck, cv, bias, bs, n_tok)

    xp = _mix_out(o_p, w_na_o, xp, mod_p[0], g_norm[0])
    xs = _mix_out(o_s, w_na_o, xs, mod_s[0], g_norm[0])
    xp = _ffn(xp, mod_p[0], g_norm[0], w_ffn_in, w_ffn_out, 0)
    xs = _ffn(xs, mod_s[0], g_norm[0], w_ffn_in, w_ffn_out, 0)

    proj_p = _ret_in_proj(xp, mod_p[1], g_norm[1], w_ret_in)
    proj_s = _ret_in_proj(xs, mod_s[1], g_norm[1], w_ret_in, _rope_tables(n_tok, RET_DK), n_tok)
    y_p, sf, sb = _retention(proj_p, ret_decay_logit[0], seq, 16, emit_state=True)
    (y_s,) = _retention(proj_s, ret_decay_logit[0], n_tok, 1,
                        state_f=state_ret_fwd[:, 0], state_b=state_ret_bwd[:, 0])

    xp = _mix_out(y_p, w_ret_o, xp, mod_p[1], g_norm[1])
    xs = _mix_out(y_s, w_ret_o, xs, mod_s[1], g_norm[1])
    xp = _ffn(xp, mod_p[1], g_norm[1], w_ffn_in, w_ffn_out, 1)
    xs = _ffn(xs, mod_s[1], g_norm[1], w_ffn_in, w_ffn_out, 1)

    kv_shape = (bp, 1, seq, NA_HEADS, NA_HEAD_DIM)
    return (xp.reshape(bp, seq, d), xs.reshape(bs, n_tok, d),
            k_p.reshape(kv_shape), v_p.reshape(kv_shape),
            sf.astype(x_prompt.dtype), sb.astype(x_prompt.dtype))
```

```python
import functools

import jax
import jax.numpy as jnp
from jax import lax
from jax.experimental import pallas as pl
from jax.experimental.pallas import tpu as pltpu

D_MODEL = 2048
GRID_W = 64
NA_HEADS = 16
NA_HEAD_DIM = D_MODEL // NA_HEADS
NA_KH = 8
NA_KW = 16
RET_HEADS = 8
RET_DK = D_MODEL // RET_HEADS
RET_CHUNK = 128
FFN_HIDDEN = -(-8 * D_MODEL // (3 * 256)) * 256
ROPE_BASE = 10000.0
EPS = 1e-6

F32 = jnp.float32
BF16 = jnp.bfloat16
MASK_VALUE = -1e30
LOG2E = 1.4426950408889634

VMEM_LIMIT_BYTES = 56 * 1024 * 1024
FFN_VMEM_LIMIT_BYTES = 60 * 1024 * 1024
ROW_CHUNK = 256
NA_QROWS = 4
NA_QBLK = NA_QROWS * GRID_W
NA_BAND_BLKS = 3


def _cparams(n_axes, vmem_limit_bytes=VMEM_LIMIT_BYTES):
    return pltpu.CompilerParams(
        dimension_semantics=("arbitrary",) * n_axes,
        vmem_limit_bytes=vmem_limit_bytes)


def _silu(x):
    return x / (1.0 + jnp.exp(-x))


def _rms(x, g):
    ms = jnp.mean(x * x, axis=-1, keepdims=True)
    return x * lax.rsqrt(ms + EPS) * g


def _dot(a, b):
    return jnp.dot(a, b, preferred_element_type=F32)


def _dot_nt(a, b):
    return lax.dot_general(a, b, (((1,), (1,)), ((), ())), preferred_element_type=F32)


def _dot_tn(a, b):
    return lax.dot_general(a, b, (((0,), (0,)), ((), ())), preferred_element_type=F32)


def _ada_kernel(cond_ref, w_ref, b_ref, o_ref):
    a = _silu(cond_ref[...]).astype(BF16)
    o_ref[0] = _dot(a, w_ref[0].astype(BF16)) + b_ref[0]


def _ada_modulation(cond, w_ada, b_ada, tn=1024):
    depth, d, n = w_ada.shape
    rows = cond.shape[0]
    return pl.pallas_call(
        _ada_kernel,
        grid=(depth, n // tn),
        in_specs=[
            pl.BlockSpec((rows, d), lambda l, j: (0, 0)),
            pl.BlockSpec((1, d, tn), lambda l, j: (l, 0, j)),
            pl.BlockSpec((1, 1, tn), lambda l, j: (l, 0, j)),
        ],
        out_specs=pl.BlockSpec((1, rows, tn), lambda l, j: (l, 0, j)),
        out_shape=jax.ShapeDtypeStruct((depth, rows, n), F32),
        compiler_params=_cparams(2),
        name="ada_modulation",
    )(cond, w_ada, b_ada.reshape(depth, 1, n))


def _row_chunks(n_rows):
    return [slice(r * ROW_CHUNK, (r + 1) * ROW_CHUNK) for r in range(n_rows // ROW_CHUNK)]


def _qkv_kernel(x_ref, mod_ref, g_ref, w_ref, *rest, nq, nk, emit_kv):
    if emit_kv:
        o_ref, k_ref, v_ref, h_scr = rest
    else:
        o_ref, h_scr = rest
    j = pl.program_id(1)
    chunks = _row_chunks(x_ref.shape[0])
    q_scale = NA_HEAD_DIM ** -0.5 * LOG2E

    @pl.when(j == 0)
    def _():
        g, sc, sh = g_ref[0:1, :], mod_ref[0, 1:2, :], mod_ref[0, 0:1, :]
        for sl in chunks:
            h = (_rms(x_ref[sl, :], g) * (1.0 + sc) + sh).astype(BF16)
            h_scr[sl, :] = h
            o_ref[sl, :] = (_dot(h, w_ref[...]) * q_scale).astype(BF16)

    @pl.when((j > 0) & (j < nq))
    def _():
        for sl in chunks:
            o_ref[sl, :] = (_dot(h_scr[sl, :], w_ref[...]) * q_scale).astype(BF16)

    @pl.when((j >= nq) & (j < nq + nk))
    def _():
        for sl in chunks:
            acc = _dot(h_scr[sl, :], w_ref[...])
            o_ref[sl, :] = acc.astype(BF16)
            if emit_kv:
                k_ref[sl, :] = acc

    @pl.when(j >= nq + nk)
    def _():
        for sl in chunks:
            acc = _dot(h_scr[sl, :], w_ref[...])
            o_ref[sl, :] = acc.astype(BF16)
            if emit_kv:
                v_ref[sl, :] = acc


def _qkv_proj(x, mod, g, w, *, emit_kv):
    t, d = x.shape
    tm = 1024
    tn = 1024 if emit_kv else 2048
    n = w.shape[1]
    groups = mod.shape[0]
    tiles_per_group = t // groups // tm
    nq = d // tn
    nk = d // tn
    out_shape = [jax.ShapeDtypeStruct((t, n), BF16)]
    out_specs = [pl.BlockSpec((tm, tn), lambda i, j: (i, j))]
    if emit_kv:
        out_shape += [jax.ShapeDtypeStruct((t, d), F32)] * 2
        out_specs += [
            pl.BlockSpec((tm, tn), lambda i, j: (i, jnp.clip(j - nq, 0, nk - 1))),
            pl.BlockSpec((tm, tn), lambda i, j: (i, jnp.clip(j - nq - nk, 0, nk - 1))),
        ]
    return pl.pallas_call(
        functools.partial(_qkv_kernel, nq=nq, nk=nk, emit_kv=emit_kv),
        grid=(t // tm, n // tn),
        in_specs=[
            pl.BlockSpec((tm, d), lambda i, j: (i, 0)),
            pl.BlockSpec((1, 6, d), lambda i, j: (i // tiles_per_group, 0, 0)),
            pl.BlockSpec((4, d), lambda i, j: (0, 0)),
            pl.BlockSpec((d, tn), lambda i, j: (0, j)),
        ],
        out_specs=out_specs,
        out_shape=out_shape,
        scratch_shapes=[pltpu.VMEM((tm, d), BF16)],
        compiler_params=_cparams(2),
        name="qkv_proj_kv" if emit_kv else "qkv_proj",
    )(x, mod, g, w)


def _pipelined_heads(n_heads, n_blk, score_blk, value_blk, store):
    s_new = p_new = l_new = None
    for step in range(n_heads + 2):
        h_s, h_e, h_o = step, step - 1, step - 2
        s_cur, s_new = s_new, []
        p_cur, p_new = p_new, []
        l_cur, l_new = l_new, None
        exp_on = 0 <= h_e < n_heads
        out_on = 0 <= h_o < n_heads
        if exp_on:
            m = s_cur[0].max(axis=0, keepdims=True)
            for sj in s_cur[1:]:
                m = jnp.maximum(m, sj.max(axis=0, keepdims=True))
        o_t = None
        for jb in range(n_blk):
            if h_s < n_heads:
                s_new.append(score_blk(h_s, jb))
            if exp_on:
                p = jnp.exp2(s_cur[jb] - m)
                part = p.sum(axis=0, keepdims=True)
                l_new = part if l_new is None else l_new + part
                p_new.append(p.astype(BF16))
            if out_on:
                part = _dot_tn(value_blk(h_o, jb), p_cur[jb])
                o_t = part if o_t is None else o_t + part
        if out_on:
            store(h_o, (o_t * (1.0 / l_cur)).T)


def _ctx_attn_kernel(q_ref, k_ref, v_ref, o_ref):
    def cols(h):
        return slice(h * NA_HEAD_DIM, (h + 1) * NA_HEAD_DIM)

    def scores(h):
        return _dot_nt(q_ref[:, cols(h)], k_ref[:, cols(h)])

    s_next = scores(0)
    for h in range(NA_HEADS):
        s = s_next
        if h + 1 < NA_HEADS:
            s_next = scores(h + 1)
        m = jnp.max(s, axis=-1, keepdims=True)
        p = jnp.exp2(s - m)
        l = jnp.sum(p, axis=-1, keepdims=True)
        o = _dot(p.astype(BF16), v_ref[:, cols(h)])
        o_ref[:, cols(h)] = (o * (1.0 / l)).astype(BF16)


def _ctx_attention(qkv, seq):
    t = qkv.shape[0]
    d = D_MODEL
    return pl.pallas_call(
        _ctx_attn_kernel,
        grid=(t // seq,),
        in_specs=[
            pl.BlockSpec((seq, d), lambda b: (b, 0)),
            pl.BlockSpec((seq, d), lambda b: (b, 1)),
            pl.BlockSpec((seq, d), lambda b: (b, 2)),
        ],
        out_specs=pl.BlockSpec((seq, d), lambda b: (b, 0)),
        out_shape=jax.ShapeDtypeStruct((t, d), BF16),
        compiler_params=_cparams(1),
        name="ctx_attention",
    )(qkv, qkv, qkv)


def _band_start_blk(rb, n_rb):
    return jnp.clip(rb - 1, 0, n_rb - NA_BAND_BLKS)


def _bias_tile_plan(n_grid_rows):
    n_rb = n_grid_rows // NA_QROWS
    plans = []
    for rb in (0, 1, n_rb - 1):
        r0 = rb * NA_QROWS
        a = min(max(rb - 1, 0), n_rb - NA_BAND_BLKS) * NA_QROWS
        rows = []
        for j in range(NA_BAND_BLKS * NA_QROWS):
            kr = a + j
            pairs = []
            for ip in range(NA_QROWS // 2):
                drs = []
                for i in (2 * ip, 2 * ip + 1):
                    r = r0 + i
                    rs = min(max(r - NA_KH // 2, 0), n_grid_rows - NA_KH)
                    drs.append(kr - r + NA_KH - 1 if rs <= kr < rs + NA_KH else None)
                pairs.append(tuple(drs))
            rows.append(pairs)
        plans.append(rows)
    return plans


def _na_bias_kernel(rpb_ref, o_ref, *, plan):
    h = pl.program_id(0)
    n_dr = 2 * NA_KH - 1
    n_dc = 2 * NA_KW - 1
    shape = (GRID_W, 2 * GRID_W)
    kc = lax.broadcasted_iota(jnp.int32, shape, 0)
    lane = lax.broadcasted_iota(jnp.int32, shape, 1)
    qc = lane & (GRID_W - 1)
    right = lane >= GRID_W
    diff = kc - qc
    cstart = jnp.clip(qc - NA_KW // 2, 0, GRID_W - NA_KW)
    in_window = (kc >= cstart) & (kc < cstart + NA_KW)
    masked = jnp.full(shape, MASK_VALUE, F32)

    used = sorted({dr for rows in plan for pairs in rows for pr in pairs for dr in pr if dr is not None})
    tiles = {dr: masked for dr in used}
    for dc in range(n_dc):
        hit = in_window & (diff == dc - (NA_KW - 1))
        for dr in used:
            bias = rpb_ref[h * (n_dr * n_dc) + dr * n_dc + dc] * LOG2E
            tiles[dr] = jnp.where(hit, bias, tiles[dr])
    tiles[None] = masked

    for ty, rows in enumerate(plan):
        for j, pairs in enumerate(rows):
            for ip, (dl, dr) in enumerate(pairs):
                tile = tiles[dl] if dl == dr else jnp.where(right, tiles[dr], tiles[dl])
                o_ref[ty, 0, j * GRID_W:(j + 1) * GRID_W, ip * 2 * GRID_W:(ip + 1) * 2 * GRID_W] = tile


def _na_bias_table(rpb, n_grid_rows):
    heads = rpb.shape[0]
    band = NA_BAND_BLKS * NA_QBLK
    plan = _bias_tile_plan(n_grid_rows)
    return pl.pallas_call(
        functools.partial(_na_bias_kernel, plan=plan),
        grid=(heads,),
        in_specs=[pl.BlockSpec(memory_space=pltpu.SMEM)],
        out_specs=pl.BlockSpec((3, 1, band, NA_QBLK), lambda h: (0, h, 0, 0)),
        out_shape=jax.ShapeDtypeStruct((3, heads, band, NA_QBLK), F32),
        compiler_params=_cparams(1),
        name="na_bias_table",
    )(rpb.reshape(-1))


def _na_attn_kernel(q_ref, k0_ref, k1_ref, k2_ref, v0_ref, v1_ref, v2_ref,
                    ck_ref, cv_ref, bias_ref, o_ref):
    dh = NA_HEAD_DIM
    n_heads = q_ref.shape[1] // dh

    n_blk = NA_BAND_BLKS + 1
    k_refs = (k0_ref, k1_ref, k2_ref)
    v_refs = (v0_ref, v1_ref, v2_ref)

    def head_cols(hh):
        return slice(hh * dh, (hh + 1) * dh)

    def score_blk(hh, jb):
        sl = head_cols(hh)
        if jb < NA_BAND_BLKS:
            return (_dot_nt(k_refs[jb][:, sl], q_ref[:, sl])
                    + bias_ref[0, hh, jb * NA_QBLK:(jb + 1) * NA_QBLK, :])
        return _dot_nt(ck_ref[0, :, sl].astype(BF16), q_ref[:, sl])

    def value_blk(hh, jb):
        sl = head_cols(hh)
        return v_refs[jb][:, sl] if jb < NA_BAND_BLKS else cv_ref[0, :, sl].astype(BF16)

    def store(hh, o):
        o_ref[:, head_cols(hh)] = o.astype(BF16)

    _pipelined_heads(n_heads, n_blk, score_blk, value_blk, store)


def _na_attention(qkv, ck, cv, bias, batch, n_tok, heads_per_step=16):
    d = D_MODEL
    width = heads_per_step * NA_HEAD_DIM
    n_hg = NA_HEADS // heads_per_step
    n_rb = n_tok // NA_QBLK
    past = ck.shape[1]

    def qmap(hg, b, rb):
        return (b * n_rb + rb, hg)

    def band_map(col0, off):
        def f(hg, b, rb):
            return (b * n_rb + _band_start_blk(rb, n_rb) + off, col0 + hg)
        return f

    def bias_map(hg, b, rb):
        ty = jnp.where(rb == 0, 0, jnp.where(rb == n_rb - 1, 2, 1))
        return (ty, hg, 0, 0)

    blk = pl.BlockSpec((NA_QBLK, width), qmap)
    in_specs = [blk]
    in_specs += [pl.BlockSpec((NA_QBLK, width), band_map(n_hg, off)) for off in range(NA_BAND_BLKS)]
    in_specs += [pl.BlockSpec((NA_QBLK, width), band_map(2 * n_hg, off)) for off in range(NA_BAND_BLKS)]
    in_specs += [pl.BlockSpec((1, past, width), lambda hg, b, rb: (b, 0, hg))] * 2
    in_specs += [pl.BlockSpec((1, heads_per_step, NA_BAND_BLKS * NA_QBLK, NA_QBLK), bias_map)]
    return pl.pallas_call(
        _na_attn_kernel,
        grid=(n_hg, batch, n_rb),
        in_specs=in_specs,
        out_specs=blk,
        out_shape=jax.ShapeDtypeStruct((batch * n_tok, d), BF16),
        compiler_params=_cparams(3),
        name="na_attention",
    )(qkv, qkv, qkv, qkv, qkv, qkv, qkv, ck, cv, bias)


def _mix_out_kernel(o_ref, w_ref, x_ref, mod_ref, g_ref, x1_ref):
    ga = mod_ref[0, 2:3, :]
    chunks = _row_chunks(o_ref.shape[0])
    y_next = _dot(o_ref[chunks[0], :], w_ref[...])
    for idx, sl in enumerate(chunks):
        y = y_next
        if idx + 1 < len(chunks):
            y_next = _dot(o_ref[chunks[idx + 1], :], w_ref[...])
        x1_ref[sl, :] = x_ref[sl, :] + ga * _rms(y, g_ref[1:2, :])


def _mix_out(o, w, x, mod, g, tm=512):
    t, d = x.shape
    groups = mod.shape[0]
    tiles_per_group = t // groups // tm
    return pl.pallas_call(
        _mix_out_kernel,
        grid=(t // tm,),
        in_specs=[
            pl.BlockSpec((tm, d), lambda i: (i, 0)),
            pl.BlockSpec((d, d), lambda i: (0, 0)),
            pl.BlockSpec((tm, d), lambda i: (i, 0)),
            pl.BlockSpec((1, 6, d), lambda i: (i // tiles_per_group, 0, 0)),
            pl.BlockSpec((4, d), lambda i: (0, 0)),
        ],
        out_specs=pl.BlockSpec((tm, d), lambda i: (i, 0)),
        out_shape=jax.ShapeDtypeStruct((t, d), F32),
        compiler_params=_cparams(1),
        name="mix_out",
    )(o, w, x, mod, g)


def _ffn_kernel(x_ref, mod_ref, g_ref, wg_ref, wu_ref, wo_ref, x2_ref, h_scr):
    c = pl.program_id(1)
    last = pl.num_programs(1) - 1
    row_chunks = _row_chunks(x_ref.shape[0])

    def hidden(h):
        return (_silu(_dot(h, wg_ref[0])) * _dot(h, wu_ref[0])).astype(BF16)

    def over_chunks(get_h, emit):
        a_next = hidden(get_h(row_chunks[0]))
        w_down = wo_ref[0].astype(BF16)
        for idx, sl in enumerate(row_chunks):
            a = a_next
            if idx + 1 < len(row_chunks):
                a_next = hidden(get_h(row_chunks[idx + 1]))
            emit(sl, _dot(a, w_down))

    @pl.when(c == 0)
    def _():
        g, sh, sc = g_ref[2:3, :], mod_ref[0, 3:4, :], mod_ref[0, 4:5, :]

        def normed(sl):
            h = (_rms(x_ref[sl, :], g) * (1.0 + sc) + sh).astype(BF16)
            h_scr[sl, :] = h
            return h

        def assign(sl, part):
            x2_ref[sl, :] = part

        over_chunks(normed, assign)

    @pl.when((c > 0) & (c < last))
    def _():
        def accumulate(sl, part):
            x2_ref[sl, :] += part

        over_chunks(lambda sl: h_scr[sl, :], accumulate)

    @pl.when(c == last)
    def _():
        ga = mod_ref[0, 5:6, :]

        def finish(sl, part):
            y = x2_ref[sl, :] + part
            x2_ref[sl, :] = x_ref[sl, :] + ga * _rms(y, g_ref[3:4, :])

        over_chunks(lambda sl: h_scr[sl, :], finish)


def _ffn(x, mod, g, w_in, w_out, layer, tm=1024, tf=512):
    t, d = x.shape
    hidden = w_out.shape[1]
    n_c = hidden // tf
    groups = mod.shape[0]
    tiles_per_group = t // groups // tm
    row = pl.BlockSpec((tm, d), lambda i, c: (i, 0))
    return pl.pallas_call(
        _ffn_kernel,
        grid=(t // tm, n_c),
        in_specs=[
            row,
            pl.BlockSpec((1, 6, d), lambda i, c: (i // tiles_per_group, 0, 0)),
            pl.BlockSpec((4, d), lambda i, c: (0, 0)),
            pl.BlockSpec((1, d, tf), lambda i, c: (layer, 0, c)),
            pl.BlockSpec((1, d, tf), lambda i, c: (layer, 0, n_c + c)),
            pl.BlockSpec((1, tf, d), lambda i, c: (layer, c, 0)),
        ],
        out_specs=row,
        out_shape=jax.ShapeDtypeStruct((t, d), F32),
        scratch_shapes=[pltpu.VMEM((tm, d), BF16)],
        compiler_params=_cparams(2, FFN_VMEM_LIMIT_BYTES),
        name="ffn",
    )(x, mod, g, w_in, w_in, w_out)


def _ret_in_kernel(x_ref, mod_ref, g_ref, w_ref, *rest, rope, n_qk, n_q, n_v):
    if rope:
        cos_ref, sin_ref, o_ref, h_scr = rest
    else:
        o_ref, h_scr = rest
    j = pl.program_id(1)
    tn = w_ref.shape[1]
    half = RET_DK // 2
    chunks = _row_chunks(x_ref.shape[0])

    def store_qk(sl, acc, scale):
        for hh in range(tn // RET_DK):
            cols = slice(hh * RET_DK, (hh + 1) * RET_DK)
            x = acc[:, cols]
            if rope:
                rot = jnp.concatenate(
                    [pltpu.roll(x[:, :half], half // 2, 1), pltpu.roll(x[:, half:], half // 2, 1)], axis=1)
                x = x * cos_ref[sl, :] + rot * sin_ref[sl, :]
            o_ref[sl, cols] = (x * scale).astype(BF16)

    @pl.when(j == 0)
    def _():
        g, sc, sh = g_ref[0:1, :], mod_ref[0, 1:2, :], mod_ref[0, 0:1, :]
        for sl in chunks:
            h = (_rms(x_ref[sl, :], g) * (1.0 + sc) + sh).astype(BF16)
            h_scr[sl, :] = h
            store_qk(sl, _dot(h, w_ref[...]), RET_DK ** -0.5)

    @pl.when((j > 0) & (j < n_qk))
    def _():
        scale = jnp.where(j < n_q, RET_DK ** -0.5, 1.0).astype(F32)
        for sl in chunks:
            store_qk(sl, _dot(h_scr[sl, :], w_ref[...]), scale)

    @pl.when((j >= n_qk) & (j < n_qk + n_v))
    def _():
        for sl in chunks:
            o_ref[sl, :] = _dot(h_scr[sl, :], w_ref[...]).astype(BF16)

    @pl.when(j >= n_qk + n_v)
    def _():
        for sl in chunks:
            o_ref[sl, :] = _silu(_dot(h_scr[sl, :], w_ref[...])).astype(BF16)


def _ret_in_proj(x, mod, g, w, rope_tables=None, seq_len=None, tm=1024, tn=2048):
    t, d = x.shape
    n = w.shape[1]
    rope = rope_tables is not None
    groups = mod.shape[0]
    tiles_per_group = t // groups // tm
    in_specs = [
        pl.BlockSpec((tm, d), lambda i, j: (i, 0)),
        pl.BlockSpec((1, 6, d), lambda i, j: (i // tiles_per_group, 0, 0)),
        pl.BlockSpec((4, d), lambda i, j: (0, 0)),
        pl.BlockSpec((d, tn), lambda i, j: (0, j)),
    ]
    args = [x, mod, g, w]
    if rope:
        tiles_per_seq = seq_len // tm
        tab = pl.BlockSpec((tm, RET_DK), lambda i, j: (i % tiles_per_seq, 0))
        in_specs += [tab, tab]
        args += list(rope_tables)
    return pl.pallas_call(
        functools.partial(_ret_in_kernel, rope=rope, n_qk=2 * d // tn, n_q=d // tn, n_v=d // tn),
        grid=(t // tm, n // tn),
        in_specs=in_specs,
        out_specs=pl.BlockSpec((tm, tn), lambda i, j: (i, j)),
        out_shape=jax.ShapeDtypeStruct((t, n), BF16),
        scratch_shapes=[pltpu.VMEM((tm, d), BF16)],
        compiler_params=_cparams(2),
        name="ret_in_rope" if rope else "ret_in",
    )(*args)


def _rope_tables(n_tokens, dim):
    n_rows = n_tokens // GRID_W
    half = dim // 2
    quarter = dim // 4
    inv = jnp.power(ROPE_BASE, -jnp.arange(0, half, 2, dtype=F32) / half)
    ang_r = jnp.arange(n_rows, dtype=F32)[:, None] * inv[None, :]
    ang_c = jnp.arange(GRID_W, dtype=F32)[:, None] * inv[None, :]

    def table(fn, lo_sign):
        by_row = jnp.broadcast_to(fn(ang_r)[:, None, :], (n_rows, GRID_W, quarter))
        by_col = jnp.broadcast_to(fn(ang_c)[None, :, :], (n_rows, GRID_W, quarter))
        full = jnp.concatenate([lo_sign * by_row, by_row, lo_sign * by_col, by_col], axis=-1)
        return full.reshape(n_tokens, dim)

    return table(jnp.cos, 1.0), table(jnp.sin, -1.0)


def _log_sigmoid(x):
    y = -x
    return -(jnp.maximum(y, 0.0) + jnp.log1p(jnp.exp(-jnp.abs(y))))


def _group_norm(o):
    mu = jnp.mean(o, axis=-1, keepdims=True)
    dev = o - mu
    var = jnp.mean(dev * dev, axis=-1, keepdims=True)
    return dev * lax.rsqrt(var + EPS)


def _ret_kernel(dl_ref, q_ref, k_ref, v_ref, gf_ref, gb_ref, *rest,
                n_seq, seq_len, has_state, emit_state, unroll):
    rest = list(rest)
    if has_state:
        s0f_ref, s0b_ref = rest[:2]
        rest = rest[2:]
    y_ref = rest.pop(0)
    if emit_state:
        sf_out_ref, sb_out_ref = rest[:2]
        rest = rest[2:]
    of_scr, pb_scr, st_scr = rest

    head = pl.program_id(1)
    C = RET_CHUNK
    dk = RET_DK
    n_chunks = seq_len // C

    def lg(shape, direction):
        return _log_sigmoid(jnp.full(shape, dl_ref[direction, head], F32))

    ii = lax.broadcasted_iota(jnp.int32, (C, C), 0)
    jj = lax.broadcasted_iota(jnp.int32, (C, C), 1)
    causal = ii >= jj
    anti = jj >= ii
    dist = jnp.abs(ii - jj).astype(F32)
    decay_f = jnp.where(causal, jnp.exp(jnp.where(causal, dist, 0.0) * lg((C, C), 0)), 0.0)
    decay_b = jnp.where(anti, jnp.exp(jnp.where(anti, dist, 0.0) * lg((C, C), 1)), 0.0)
    pos = lax.broadcasted_iota(jnp.int32, (C, dk), 0).astype(F32)
    lgf = lg((C, dk), 0)
    lgb = lg((C, dk), 1)
    qd_f = jnp.exp((pos + 1.0) * lgf)
    kd_f = jnp.exp((C - 1.0 - pos) * lgf)
    qd_b = jnp.exp((C - pos) * lgb)
    kd_b = jnp.exp(pos * lgb)
    cd_f = jnp.exp(C * lg((1, dk), 0))
    cd_b = jnp.exp(C * lg((1, dk), 1))

    def chunk_rows(s, t):
        return pl.ds(pl.multiple_of(s * seq_len + t * C, C), C)

    def key_value(k, v, kd):
        return _dot_tn((k.astype(F32) * kd).astype(BF16), v)

    def fwd_local(s, t):
        rows = chunk_rows(s, t)
        q, k, v = q_ref[rows, :], k_ref[rows, :], v_ref[rows, :]
        raw = _dot_nt(q, k)
        pb_scr[rows, :] = (raw * decay_b).astype(BF16)
        return rows, q, _dot((raw * decay_f).astype(BF16), v), key_value(k, v, kd_f)

    def fwd_finish(local, state):
        rows, q, inner, kv = local
        of_scr[rows, :] = _group_norm(inner + _dot(q, state.astype(BF16)) * qd_f)
        return state * cd_f + kv

    def bwd_local(s, t):
        rows = chunk_rows(s, t)
        q, k, v = q_ref[rows, :], k_ref[rows, :], v_ref[rows, :]
        return rows, q, _dot(pb_scr[rows, :], v), key_value(k, v, kd_b)

    def bwd_finish(local, state):
        rows, q, inner, kv = local
        o = inner + _dot(q, state.astype(BF16)) * qd_b
        y = (gf_ref[rows, :].astype(F32) * of_scr[rows, :]
             + gb_ref[rows, :].astype(F32) * _group_norm(o))
        y_ref[rows, :] = y.astype(BF16)
        return state * cd_b + kv

    def sweep(local_fn, finish_fn, reverse):
        group = min(unroll, n_chunks)

        def run_group(gi):
            items = [(s, gi * group + c) for s in range(n_seq) for c in range(group)]
            states = {}

            def local_at(item):
                s, t = item
                return local_fn(s, n_chunks - 1 - t if reverse else t)

            nxt = local_at(items[0])
            for idx, (s, t) in enumerate(items):
                cur = nxt
                if idx + 1 < len(items):
                    nxt = local_at(items[idx + 1])
                state = states[s] if s in states else st_scr[s]
                states[s] = finish_fn(cur, state)
            for s in range(n_seq):
                st_scr[s] = states[s]

        n_groups = n_chunks // group
        if n_groups == 1:
            run_group(0)
        else:
            lax.fori_loop(0, n_groups, lambda gi, carry: (run_group(gi), carry)[1], 0)

    for s in range(n_seq):
        st_scr[s] = s0f_ref[s, 0] if has_state else jnp.zeros((dk, dk), F32)
    sweep(fwd_local, fwd_finish, reverse=False)
    for s in range(n_seq):
        if emit_state:
            sf_out_ref[s, 0, 0] = st_scr[s]
        st_scr[s] = s0b_ref[s, 0] if has_state else jnp.zeros((dk, dk), F32)
    sweep(bwd_local, bwd_finish, reverse=True)
    if emit_state:
        for s in range(n_seq):
            sb_out_ref[s, 0, 0] = st_scr[s]


def _retention(proj, decay_logit, seq_len, n_seq, state_f=None, state_b=None, emit_state=False):
    t = proj.shape[0]
    d = D_MODEL
    dk = RET_DK
    heads = RET_HEADS
    rows = n_seq * seq_len
    has_state = state_f is not None

    def col(offset):
        return pl.BlockSpec((rows, dk), lambda b, h: (b, offset * heads + h))

    in_specs = [pl.BlockSpec(memory_space=pltpu.SMEM)] + [col(o) for o in range(5)]
    args = [decay_logit, proj, proj, proj, proj, proj]
    if has_state:
        st = pl.BlockSpec((n_seq, 1, dk, dk), lambda b, h: (b, h, 0, 0))
        in_specs += [st, st]
        args += [state_f, state_b]
    out_specs = [pl.BlockSpec((rows, dk), lambda b, h: (b, h))]
    out_shape = [jax.ShapeDtypeStruct((t, d), BF16)]
    if emit_state:
        n_batch = t // seq_len
        st_out = pl.BlockSpec((n_seq, 1, 1, dk, dk), lambda b, h: (b, 0, h, 0, 0))
        out_specs += [st_out, st_out]
        out_shape += [jax.ShapeDtypeStruct((n_batch, 1, heads, dk, dk), F32)] * 2
    return pl.pallas_call(
        functools.partial(_ret_kernel, n_seq=n_seq, seq_len=seq_len,
                          has_state=has_state, emit_state=emit_state, unroll=32),
        grid=(t // rows, heads),
        in_specs=in_specs,
        out_specs=out_specs,
        out_shape=out_shape,
        scratch_shapes=[pltpu.VMEM((rows, dk), F32), pltpu.VMEM((rows, RET_CHUNK), BF16),
                        pltpu.VMEM((n_seq, dk, dk), F32)],
        compiler_params=_cparams(2),
        name="retention_state" if emit_state else "retention",
    )(*args)


def kernel(x_prompt, x_sample, cache_na_k, cache_na_v, state_ret_fwd, state_ret_bwd, c, c_ctx,
           w_ada, b_ada, g_norm, na_w_qkv, na_w_o, na_rpb, ret_w_in, ret_w_o, ret_decay_logit,
           ffn_w_in, ffn_w_out):
    bp, seq, d = x_prompt.shape
    bs, n_tok, _ = x_sample.shape
    past = cache_na_k.shape[2]
    xp = x_prompt.reshape(bp * seq, d)
    xs = x_sample.reshape(bs * n_tok, d)

    cond = jnp.concatenate([c_ctx[None, :], c, jnp.zeros((8 - 1 - bs, d), F32)], axis=0)
    mod = _ada_modulation(cond, w_ada, b_ada).reshape(w_ada.shape[0], 8, 6, d)
    mod_p = [mod[l, 0:1] for l in range(2)]
    mod_s = [mod[l, 1:1 + bs] for l in range(2)]

    w_qkv = na_w_qkv[0].astype(BF16)
    w_na_o = na_w_o[0].astype(BF16)
    w_ret_in = ret_w_in[0].astype(BF16)
    w_ret_o = ret_w_o[0].astype(BF16)
    w_ffn_in = ffn_w_in.astype(BF16)
    w_ffn_out = ffn_w_out

    qkv_p, k_p, v_p = _qkv_proj(xp, mod_p[0], g_norm[0], w_qkv, emit_kv=True)
    (qkv_s,) = _qkv_proj(xs, mod_s[0], g_norm[0], w_qkv, emit_kv=False)
    o_p = _ctx_attention(qkv_p, seq)
    bias = _na_bias_table(na_rpb[0], n_tok // GRID_W)
    ck = cache_na_k[:, 0].reshape(bs, past, d)
    cv = cache_na_v[:, 0].reshape(bs, past, d)
    o_s = _na_attention(qkv_s, ck, cv, bias, bs, n_tok)

    xp = _mix_out(o_p, w_na_o, xp, mod_p[0], g_norm[0])
    xs = _mix_out(o_s, w_na_o, xs, mod_s[0], g_norm[0])
    xp = _ffn(xp, mod_p[0], g_norm[0], w_ffn_in, w_ffn_out, 0)
    xs = _ffn(xs, mod_s[0], g_norm[0], w_ffn_in, w_ffn_out, 0)

    proj_p = _ret_in_proj(xp, mod_p[1], g_norm[1], w_ret_in)
    proj_s = _ret_in_proj(xs, mod_s[1], g_norm[1], w_ret_in, _rope_tables(n_tok, RET_DK), n_tok)
    y_p, sf, sb = _retention(proj_p, ret_decay_logit[0], seq, 16, emit_state=True)
    (y_s,) = _retention(proj_s, ret_decay_logit[0], n_tok, 1,
                        state_f=state_ret_fwd[:, 0], state_b=state_ret_bwd[:, 0])

    xp = _mix_out(y_p, w_ret_o, xp, mod_p[1], g_norm[1])
    xs = _mix_out(y_s, w_ret_o, xs, mod_s[1], g_norm[1])
    xp = _ffn(xp, mod_p[1], g_norm[1], w_ffn_in, w_ffn_out, 1)
    xs = _ffn(xs, mod_s[1], g_norm[1], w_ffn_in, w_ffn_out, 1)

    kv_shape = (bp, 1, seq, NA_HEADS, NA_HEAD_DIM)
    return (xp.reshape(bp, seq, d), xs.reshape(bs, n_tok, d),
            k_p.reshape(kv_shape), v_p.reshape(kv_shape),
            sf.astype(x_prompt.dtype), sb.astype(x_prompt.dtype))
```

```python
import functools

import jax
import jax.numpy as jnp
from jax import lax
from jax.experimental import pallas as pl
from jax.experimental.pallas import tpu as pltpu

D_MODEL = 2048
GRID_W = 64
NA_HEADS = 16
NA_HEAD_DIM = D_MODEL // NA_HEADS
NA_KH = 8
NA_KW = 16
RET_HEADS = 8
RET_DK = D_MODEL // RET_HEADS
RET_CHUNK = 128
FFN_HIDDEN = -(-8 * D_MODEL // (3 * 256)) * 256
ROPE_BASE = 10000.0
EPS = 1e-6

F32 = jnp.float32
BF16 = jnp.bfloat16
MASK_VALUE = -1e30
LOG2E = 1.4426950408889634

VMEM_LIMIT_BYTES = 56 * 1024 * 1024
FFN_VMEM_LIMIT_BYTES = 60 * 1024 * 1024
ROW_CHUNK = 256
NA_QROWS = 4
NA_QBLK = NA_QROWS * GRID_W
NA_BAND_BLKS = 3


def _cparams(n_axes, vmem_limit_bytes=VMEM_LIMIT_BYTES):
    return pltpu.CompilerParams(
        dimension_semantics=("arbitrary",) * n_axes,
        vmem_limit_bytes=vmem_limit_bytes)


def _silu(x):
    return x / (1.0 + jnp.exp(-x))


def _rms(x, g):
    ms = jnp.mean(x * x, axis=-1, keepdims=True)
    return x * lax.rsqrt(ms + EPS) * g


def _dot(a, b):
    return jnp.dot(a, b, preferred_element_type=F32)


def _dot_nt(a, b):
    return lax.dot_general(a, b, (((1,), (1,)), ((), ())), preferred_element_type=F32)


def _dot_tn(a, b):
    return lax.dot_general(a, b, (((0,), (0,)), ((), ())), preferred_element_type=F32)


def _ada_kernel(cond_ref, w_ref, b_ref, o_ref):
    a = _silu(cond_ref[...]).astype(BF16)
    o_ref[0] = _dot(a, w_ref[0].astype(BF16)) + b_ref[0]


def _ada_modulation(cond, w_ada, b_ada, tn=1024):
    depth, d, n = w_ada.shape
    rows = cond.shape[0]
    return pl.pallas_call(
        _ada_kernel,
        grid=(depth, n // tn),
        in_specs=[
            pl.BlockSpec((rows, d), lambda l, j: (0, 0)),
            pl.BlockSpec((1, d, tn), lambda l, j: (l, 0, j)),
            pl.BlockSpec((1, 1, tn), lambda l, j: (l, 0, j)),
        ],
        out_specs=pl.BlockSpec((1, rows, tn), lambda l, j: (l, 0, j)),
        out_shape=jax.ShapeDtypeStruct((depth, rows, n), F32),
        compiler_params=_cparams(2),
        name="ada_modulation",
    )(cond, w_ada, b_ada.reshape(depth, 1, n))


def _row_chunks(n_rows):
    return [slice(r * ROW_CHUNK, (r + 1) * ROW_CHUNK) for r in range(n_rows // ROW_CHUNK)]


def _qkv_kernel(x_ref, mod_ref, g_ref, w_ref, *rest, nq, nk, emit_kv):
    if emit_kv:
        o_ref, k_ref, v_ref, h_scr = rest
    else:
        o_ref, h_scr = rest
    j = pl.program_id(1)
    chunks = _row_chunks(x_ref.shape[0])
    q_scale = NA_HEAD_DIM ** -0.5 * LOG2E

    @pl.when(j == 0)
    def _():
        g, sc, sh = g_ref[0:1, :], mod_ref[0, 1:2, :], mod_ref[0, 0:1, :]
        for sl in chunks:
            h = (_rms(x_ref[sl, :], g) * (1.0 + sc) + sh).astype(BF16)
            h_scr[sl, :] = h
            o_ref[sl, :] = (_dot(h, w_ref[...]) * q_scale).astype(BF16)

    @pl.when((j > 0) & (j < nq))
    def _():
        for sl in chunks:
            o_ref[sl, :] = (_dot(h_scr[sl, :], w_ref[...]) * q_scale).astype(BF16)

    @pl.when((j >= nq) & (j < nq + nk))
    def _():
        for sl in chunks:
            acc = _dot(h_scr[sl, :], w_ref[...])
            o_ref[sl, :] = acc.astype(BF16)
            if emit_kv:
                k_ref[sl, :] = acc

    @pl.when(j >= nq + nk)
    def _():
        for sl in chunks:
            acc = _dot(h_scr[sl, :], w_ref[...])
            o_ref[sl, :] = acc.astype(BF16)
            if emit_kv:
                v_ref[sl, :] = acc


def _qkv_proj(x, mod, g, w, *, emit_kv):
    t, d = x.shape
    tm = 1024
    tn = 1024 if emit_kv else 2048
    n = w.shape[1]
    groups = mod.shape[0]
    tiles_per_group = t // groups // tm
    nq = d // tn
    nk = d // tn
    out_shape = [jax.ShapeDtypeStruct((t, n), BF16)]
    out_specs = [pl.BlockSpec((tm, tn), lambda i, j: (i, j))]
    if emit_kv:
        out_shape += [jax.ShapeDtypeStruct((t, d), F32)] * 2
        out_specs += [
            pl.BlockSpec((tm, tn), lambda i, j: (i, jnp.clip(j - nq, 0, nk - 1))),
            pl.BlockSpec((tm, tn), lambda i, j: (i, jnp.clip(j - nq - nk, 0, nk - 1))),
        ]
    return pl.pallas_call(
        functools.partial(_qkv_kernel, nq=nq, nk=nk, emit_kv=emit_kv),
        grid=(t // tm, n // tn),
        in_specs=[
            pl.BlockSpec((tm, d), lambda i, j: (i, 0)),
            pl.BlockSpec((1, 6, d), lambda i, j: (i // tiles_per_group, 0, 0)),
            pl.BlockSpec((4, d), lambda i, j: (0, 0)),
            pl.BlockSpec((d, tn), lambda i, j: (0, j)),
        ],
        out_specs=out_specs,
        out_shape=out_shape,
        scratch_shapes=[pltpu.VMEM((tm, d), BF16)],
        compiler_params=_cparams(2),
        name="qkv_proj_kv" if emit_kv else "qkv_proj",
    )(x, mod, g, w)


def _pipelined_heads(n_heads, n_blk, score_blk, value_blk, store):
    s_new = p_new = l_new = None
    for step in range(n_heads + 2):
        h_s, h_e, h_o = step, step - 1, step - 2
        s_cur, s_new = s_new, []
        p_cur, p_new = p_new, []
        l_cur, l_new = l_new, None
        exp_on = 0 <= h_e < n_heads
        out_on = 0 <= h_o < n_heads
        if exp_on:
            m = s_cur[0].max(axis=0, keepdims=True)
            for sj in s_cur[1:]:
                m = jnp.maximum(m, sj.max(axis=0, keepdims=True))
        o_t = None
        for jb in range(n_blk):
            if h_s < n_heads:
                s_new.append(score_blk(h_s, jb))
            if exp_on:
                p = jnp.exp2(s_cur[jb] - m)
                part = p.sum(axis=0, keepdims=True)
                l_new = part if l_new is None else l_new + part
                p_new.append(p.astype(BF16))
            if out_on:
                part = _dot_tn(value_blk(h_o, jb), p_cur[jb])
                o_t = part if o_t is None else o_t + part
        if out_on:
            store(h_o, (o_t * (1.0 / l_cur)).T)


def _ctx_attn_kernel(q_ref, k_ref, v_ref, o_ref, *, seq):
    items = [(slice(r * seq, (r + 1) * seq), slice(h * NA_HEAD_DIM, (h + 1) * NA_HEAD_DIM))
             for r in range(q_ref.shape[0] // seq) for h in range(NA_HEADS)]

    def scores(item):
        rows, cols = item
        return _dot_nt(q_ref[rows, cols], k_ref[rows, cols])

    def softmax(s):
        m = jnp.max(s, axis=-1, keepdims=True)
        p = jnp.exp2(s - m)
        return p.astype(BF16), jnp.sum(p, axis=-1, keepdims=True)

    def finish(item, p, l):
        rows, cols = item
        o_ref[rows, cols] = (_dot(p, v_ref[rows, cols]) * (1.0 / l)).astype(BF16)

    s_next = scores(items[0])
    pending = None
    for idx, item in enumerate(items):
        s = s_next
        if idx + 1 < len(items):
            s_next = scores(items[idx + 1])
        p, l = softmax(s)
        if pending is not None:
            finish(*pending)
        pending = (item, p, l)
    finish(*pending)


def _ctx_attention(qkv, seq, requests_per_step=2):
    t = qkv.shape[0]
    d = D_MODEL
    rows = requests_per_step * seq
    return pl.pallas_call(
        functools.partial(_ctx_attn_kernel, seq=seq),
        grid=(t // rows,),
        in_specs=[
            pl.BlockSpec((rows, d), lambda b: (b, 0)),
            pl.BlockSpec((rows, d), lambda b: (b, 1)),
            pl.BlockSpec((rows, d), lambda b: (b, 2)),
        ],
        out_specs=pl.BlockSpec((rows, d), lambda b: (b, 0)),
        out_shape=jax.ShapeDtypeStruct((t, d), BF16),
        compiler_params=_cparams(1),
        name="ctx_attention",
    )(qkv, qkv, qkv)


def _band_start_blk(rb, n_rb):
    return jnp.clip(rb - 1, 0, n_rb - NA_BAND_BLKS)


def _bias_tile_plan(n_grid_rows):
    n_rb = n_grid_rows // NA_QROWS
    plans = []
    for rb in (0, 1, n_rb - 1):
        r0 = rb * NA_QROWS
        a = min(max(rb - 1, 0), n_rb - NA_BAND_BLKS) * NA_QROWS
        rows = []
        for j in range(NA_BAND_BLKS * NA_QROWS):
            kr = a + j
            pairs = []
            for ip in range(NA_QROWS // 2):
                drs = []
                for i in (2 * ip, 2 * ip + 1):
                    r = r0 + i
                    rs = min(max(r - NA_KH // 2, 0), n_grid_rows - NA_KH)
                    drs.append(kr - r + NA_KH - 1 if rs <= kr < rs + NA_KH else None)
                pairs.append(tuple(drs))
            rows.append(pairs)
        plans.append(rows)
    return plans


def _na_bias_kernel(rpb_ref, o_ref, *, plan):
    h = pl.program_id(0)
    n_dr = 2 * NA_KH - 1
    n_dc = 2 * NA_KW - 1
    shape = (GRID_W, 2 * GRID_W)
    kc = lax.broadcasted_iota(jnp.int32, shape, 0)
    lane = lax.broadcasted_iota(jnp.int32, shape, 1)
    qc = lane & (GRID_W - 1)
    right = lane >= GRID_W
    diff = kc - qc
    cstart = jnp.clip(qc - NA_KW // 2, 0, GRID_W - NA_KW)
    in_window = (kc >= cstart) & (kc < cstart + NA_KW)
    masked = jnp.full(shape, MASK_VALUE, F32)

    used = sorted({dr for rows in plan for pairs in rows for pr in pairs for dr in pr if dr is not None})
    tiles = {dr: masked for dr in used}
    for dc in range(n_dc):
        hit = in_window & (diff == dc - (NA_KW - 1))
        for dr in used:
            bias = rpb_ref[h * (n_dr * n_dc) + dr * n_dc + dc] * LOG2E
            tiles[dr] = jnp.where(hit, bias, tiles[dr])
    tiles[None] = masked

    for ty, rows in enumerate(plan):
        for j, pairs in enumerate(rows):
            for ip, (dl, dr) in enumerate(pairs):
                tile = tiles[dl] if dl == dr else jnp.where(right, tiles[dr], tiles[dl])
                o_ref[ty, 0, j * GRID_W:(j + 1) * GRID_W, ip * 2 * GRID_W:(ip + 1) * 2 * GRID_W] = tile


def _na_bias_table(rpb, n_grid_rows):
    heads = rpb.shape[0]
    band = NA_BAND_BLKS * NA_QBLK
    plan = _bias_tile_plan(n_grid_rows)
    return pl.pallas_call(
        functools.partial(_na_bias_kernel, plan=plan),
        grid=(heads,),
        in_specs=[pl.BlockSpec(memory_space=pltpu.SMEM)],
        out_specs=pl.BlockSpec((3, 1, band, NA_QBLK), lambda h: (0, h, 0, 0)),
        out_shape=jax.ShapeDtypeStruct((3, heads, band, NA_QBLK), F32),
        compiler_params=_cparams(1),
        name="na_bias_table",
    )(rpb.reshape(-1))


def _na_attn_kernel(q_ref, k0_ref, k1_ref, k2_ref, v0_ref, v1_ref, v2_ref,
                    ck_ref, cv_ref, bias_ref, o_ref):
    dh = NA_HEAD_DIM
    n_heads = q_ref.shape[1] // dh

    n_blk = NA_BAND_BLKS + 1
    k_refs = (k0_ref, k1_ref, k2_ref)
    v_refs = (v0_ref, v1_ref, v2_ref)

    def head_cols(hh):
        return slice(hh * dh, (hh + 1) * dh)

    def score_blk(hh, jb):
        sl = head_cols(hh)
        if jb < NA_BAND_BLKS:
            return (_dot_nt(k_refs[jb][:, sl], q_ref[:, sl])
                    + bias_ref[0, hh, jb * NA_QBLK:(jb + 1) * NA_QBLK, :])
        return _dot_nt(ck_ref[0, :, sl].astype(BF16), q_ref[:, sl])

    def value_blk(hh, jb):
        sl = head_cols(hh)
        return v_refs[jb][:, sl] if jb < NA_BAND_BLKS else cv_ref[0, :, sl].astype(BF16)

    def store(hh, o):
        o_ref[:, head_cols(hh)] = o.astype(BF16)

    _pipelined_heads(n_heads, n_blk, score_blk, value_blk, store)


def _na_attention(qkv, ck, cv, bias, batch, n_tok, heads_per_step=16):
    d = D_MODEL
    width = heads_per_step * NA_HEAD_DIM
    n_hg = NA_HEADS // heads_per_step
    n_rb = n_tok // NA_QBLK
    past = ck.shape[1]

    def qmap(hg, b, rb):
        return (b * n_rb + rb, hg)

    def band_map(col0, off):
        def f(hg, b, rb):
            return (b * n_rb + _band_start_blk(rb, n_rb) + off, col0 + hg)
        return f

    def bias_map(hg, b, rb):
        ty = jnp.where(rb == 0, 0, jnp.where(rb == n_rb - 1, 2, 1))
        return (ty, hg, 0, 0)

    blk = pl.BlockSpec((NA_QBLK, width), qmap)
    in_specs = [blk]
    in_specs += [pl.BlockSpec((NA_QBLK, width), band_map(n_hg, off)) for off in range(NA_BAND_BLKS)]
    in_specs += [pl.BlockSpec((NA_QBLK, width), band_map(2 * n_hg, off)) for off in range(NA_BAND_BLKS)]
    in_specs += [pl.BlockSpec((1, past, width), lambda hg, b, rb: (b, 0, hg))] * 2
    in_specs += [pl.BlockSpec((1, heads_per_step, NA_BAND_BLKS * NA_QBLK, NA_QBLK), bias_map)]
    return pl.pallas_call(
        _na_attn_kernel,
        grid=(n_hg, batch, n_rb),
        in_specs=in_specs,
        out_specs=blk,
        out_shape=jax.ShapeDtypeStruct((batch * n_tok, d), BF16),
        compiler_params=_cparams(3),
        name="na_attention",
    )(qkv, qkv, qkv, qkv, qkv, qkv, qkv, ck, cv, bias)


def _mix_out_kernel(o_ref, w_ref, x_ref, mod_ref, g_ref, x1_ref):
    ga = mod_ref[0, 2:3, :]
    chunks = _row_chunks(o_ref.shape[0])
    y_next = _dot(o_ref[chunks[0], :], w_ref[...])
    for idx, sl in enumerate(chunks):
        y = y_next
        if idx + 1 < len(chunks):
            y_next = _dot(o_ref[chunks[idx + 1], :], w_ref[...])
        x1_ref[sl, :] = x_ref[sl, :] + ga * _rms(y, g_ref[1:2, :])


def _mix_out(o, w, x, mod, g, tm=512):
    t, d = x.shape
    groups = mod.shape[0]
    tiles_per_group = t // groups // tm
    return pl.pallas_call(
        _mix_out_kernel,
        grid=(t // tm,),
        in_specs=[
            pl.BlockSpec((tm, d), lambda i: (i, 0)),
            pl.BlockSpec((d, d), lambda i: (0, 0)),
            pl.BlockSpec((tm, d), lambda i: (i, 0)),
            pl.BlockSpec((1, 6, d), lambda i: (i // tiles_per_group, 0, 0)),
            pl.BlockSpec((4, d), lambda i: (0, 0)),
        ],
        out_specs=pl.BlockSpec((tm, d), lambda i: (i, 0)),
        out_shape=jax.ShapeDtypeStruct((t, d), F32),
        compiler_params=_cparams(1),
        name="mix_out",
    )(o, w, x, mod, g)


def _ffn_kernel(x_ref, mod_ref, g_ref, wg_ref, wu_ref, wo_ref, x2_ref, h_scr):
    c = pl.program_id(1)
    last = pl.num_programs(1) - 1
    row_chunks = _row_chunks(x_ref.shape[0])

    def hidden(h):
        return (_silu(_dot(h, wg_ref[0])) * _dot(h, wu_ref[0])).astype(BF16)

    def over_chunks(get_h, emit):
        a_next = hidden(get_h(row_chunks[0]))
        w_down = wo_ref[0].astype(BF16)
        for idx, sl in enumerate(row_chunks):
            a = a_next
            if idx + 1 < len(row_chunks):
                a_next = hidden(get_h(row_chunks[idx + 1]))
            emit(sl, _dot(a, w_down))

    @pl.when(c == 0)
    def _():
        g, sh, sc = g_ref[2:3, :], mod_ref[0, 3:4, :], mod_ref[0, 4:5, :]

        def normed(sl):
            h = (_rms(x_ref[sl, :], g) * (1.0 + sc) + sh).astype(BF16)
            h_scr[sl, :] = h
            return h

        def assign(sl, part):
            x2_ref[sl, :] = part

        over_chunks(normed, assign)

    @pl.when((c > 0) & (c < last))
    def _():
        def accumulate(sl, part):
            x2_ref[sl, :] += part

        over_chunks(lambda sl: h_scr[sl, :], accumulate)

    @pl.when(c == last)
    def _():
        ga = mod_ref[0, 5:6, :]

        def finish(sl, part):
            y = x2_ref[sl, :] + part
            x2_ref[sl, :] = x_ref[sl, :] + ga * _rms(y, g_ref[3:4, :])

        over_chunks(lambda sl: h_scr[sl, :], finish)


def _ffn(x, mod, g, w_in, w_out, layer, tm=1024, tf=512):
    t, d = x.shape
    hidden = w_out.shape[1]
    n_c = hidden // tf
    groups = mod.shape[0]
    tiles_per_group = t // groups // tm
    row = pl.BlockSpec((tm, d), lambda i, c: (i, 0))
    return pl.pallas_call(
        _ffn_kernel,
        grid=(t // tm, n_c),
        in_specs=[
            row,
            pl.BlockSpec((1, 6, d), lambda i, c: (i // tiles_per_group, 0, 0)),
            pl.BlockSpec((4, d), lambda i, c: (0, 0)),
            pl.BlockSpec((1, d, tf), lambda i, c: (layer, 0, c)),
            pl.BlockSpec((1, d, tf), lambda i, c: (layer, 0, n_c + c)),
            pl.BlockSpec((1, tf, d), lambda i, c: (layer, c, 0)),
        ],
        out_specs=row,
        out_shape=jax.ShapeDtypeStruct((t, d), F32),
        scratch_shapes=[pltpu.VMEM((tm, d), BF16)],
        compiler_params=_cparams(2, FFN_VMEM_LIMIT_BYTES),
        name="ffn",
    )(x, mod, g, w_in, w_in, w_out)


def _ret_in_kernel(x_ref, mod_ref, g_ref, w_ref, *rest, rope, n_qk, n_q, n_v):
    if rope:
        cos_ref, sin_ref, o_ref, h_scr = rest
    else:
        o_ref, h_scr = rest
    j = pl.program_id(1)
    tn = w_ref.shape[1]
    half = RET_DK // 2
    chunks = _row_chunks(x_ref.shape[0])

    def store_qk(sl, acc, scale):
        for hh in range(tn // RET_DK):
            cols = slice(hh * RET_DK, (hh + 1) * RET_DK)
            x = acc[:, cols]
            if rope:
                rot = jnp.concatenate(
                    [pltpu.roll(x[:, :half], half // 2, 1), pltpu.roll(x[:, half:], half // 2, 1)], axis=1)
                x = x * cos_ref[sl, :] + rot * sin_ref[sl, :]
            o_ref[sl, cols] = (x * scale).astype(BF16)

    @pl.when(j == 0)
    def _():
        g, sc, sh = g_ref[0:1, :], mod_ref[0, 1:2, :], mod_ref[0, 0:1, :]
        for sl in chunks:
            h = (_rms(x_ref[sl, :], g) * (1.0 + sc) + sh).astype(BF16)
            h_scr[sl, :] = h
            store_qk(sl, _dot(h, w_ref[...]), RET_DK ** -0.5)

    @pl.when((j > 0) & (j < n_qk))
    def _():
        scale = jnp.where(j < n_q, RET_DK ** -0.5, 1.0).astype(F32)
        for sl in chunks:
            store_qk(sl, _dot(h_scr[sl, :], w_ref[...]), scale)

    @pl.when((j >= n_qk) & (j < n_qk + n_v))
    def _():
        for sl in chunks:
            o_ref[sl, :] = _dot(h_scr[sl, :], w_ref[...]).astype(BF16)

    @pl.when(j >= n_qk + n_v)
    def _():
        for sl in chunks:
            o_ref[sl, :] = _silu(_dot(h_scr[sl, :], w_ref[...])).astype(BF16)


def _ret_in_proj(x, mod, g, w, rope_tables=None, seq_len=None, tm=1024, tn=2048):
    t, d = x.shape
    n = w.shape[1]
    rope = rope_tables is not None
    groups = mod.shape[0]
    tiles_per_group = t // groups // tm
    in_specs = [
        pl.BlockSpec((tm, d), lambda i, j: (i, 0)),
        pl.BlockSpec((1, 6, d), lambda i, j: (i // tiles_per_group, 0, 0)),
        pl.BlockSpec((4, d), lambda i, j: (0, 0)),
        pl.BlockSpec((d, tn), lambda i, j: (0, j)),
    ]
    args = [x, mod, g, w]
    if rope:
        tiles_per_seq = seq_len // tm
        tab = pl.BlockSpec((tm, RET_DK), lambda i, j: (i % tiles_per_seq, 0))
        in_specs += [tab, tab]
        args += list(rope_tables)
    return pl.pallas_call(
        functools.partial(_ret_in_kernel, rope=rope, n_qk=2 * d // tn, n_q=d // tn, n_v=d // tn),
        grid=(t // tm, n // tn),
        in_specs=in_specs,
        out_specs=pl.BlockSpec((tm, tn), lambda i, j: (i, j)),
        out_shape=jax.ShapeDtypeStruct((t, n), BF16),
        scratch_shapes=[pltpu.VMEM((tm, d), BF16)],
        compiler_params=_cparams(2),
        name="ret_in_rope" if rope else "ret_in",
    )(*args)


def _rope_tables(n_tokens, dim):
    n_rows = n_tokens // GRID_W
    half = dim // 2
    quarter = dim // 4
    inv = jnp.power(ROPE_BASE, -jnp.arange(0, half, 2, dtype=F32) / half)
    ang_r = jnp.arange(n_rows, dtype=F32)[:, None] * inv[None, :]
    ang_c = jnp.arange(GRID_W, dtype=F32)[:, None] * inv[None, :]

    def table(fn, lo_sign):
        by_row = jnp.broadcast_to(fn(ang_r)[:, None, :], (n_rows, GRID_W, quarter))
        by_col = jnp.broadcast_to(fn(ang_c)[None, :, :], (n_rows, GRID_W, quarter))
        full = jnp.concatenate([lo_sign * by_row, by_row, lo_sign * by_col, by_col], axis=-1)
        return full.reshape(n_tokens, dim)

    return table(jnp.cos, 1.0), table(jnp.sin, -1.0)


def _log_sigmoid(x):
    y = -x
    return -(jnp.maximum(y, 0.0) + jnp.log1p(jnp.exp(-jnp.abs(y))))


def _group_norm(o):
    mu = jnp.mean(o, axis=-1, keepdims=True)
    dev = o - mu
    var = jnp.mean(dev * dev, axis=-1, keepdims=True)
    return dev * lax.rsqrt(var + EPS)


def _ret_kernel(dl_ref, q_ref, k_ref, v_ref, gf_ref, gb_ref, *rest,
                n_seq, seq_len, has_state, emit_state, unroll):
    rest = list(rest)
    if has_state:
        s0f_ref, s0b_ref = rest[:2]
        rest = rest[2:]
    y_ref = rest.pop(0)
    if emit_state:
        sf_out_ref, sb_out_ref = rest[:2]
        rest = rest[2:]
    of_scr, pb_scr, st_scr = rest

    head = pl.program_id(1)
    C = RET_CHUNK
    dk = RET_DK
    n_chunks = seq_len // C

    def lg(shape, direction):
        return _log_sigmoid(jnp.full(shape, dl_ref[direction, head], F32))

    ii = lax.broadcasted_iota(jnp.int32, (C, C), 0)
    jj = lax.broadcasted_iota(jnp.int32, (C, C), 1)
    causal = ii >= jj
    anti = jj >= ii
    dist = jnp.abs(ii - jj).astype(F32)
    decay_f = jnp.where(causal, jnp.exp(jnp.where(causal, dist, 0.0) * lg((C, C), 0)), 0.0)
    decay_b = jnp.where(anti, jnp.exp(jnp.where(anti, dist, 0.0) * lg((C, C), 1)), 0.0)
    pos = lax.broadcasted_iota(jnp.int32, (C, dk), 0).astype(F32)
    lgf = lg((C, dk), 0)
    lgb = lg((C, dk), 1)
    qd_f = jnp.exp((pos + 1.0) * lgf)
    kd_f = jnp.exp((C - 1.0 - pos) * lgf)
    qd_b = jnp.exp((C - pos) * lgb)
    kd_b = jnp.exp(pos * lgb)
    cd_f = jnp.exp(C * lg((1, dk), 0))
    cd_b = jnp.exp(C * lg((1, dk), 1))

    def chunk_rows(s, t):
        return pl.ds(pl.multiple_of(s * seq_len + t * C, C), C)

    def key_value(k, v, kd):
        return _dot_tn((k.astype(F32) * kd).astype(BF16), v)

    def fwd_local(s, t):
        rows = chunk_rows(s, t)
        q, k, v = q_ref[rows, :], k_ref[rows, :], v_ref[rows, :]
        raw = _dot_nt(q, k)
        pb_scr[rows, :] = (raw * decay_b).astype(BF16)
        return rows, q, _dot((raw * decay_f).astype(BF16), v), key_value(k, v, kd_f)

    def fwd_finish(local, state):
        rows, q, inner, kv = local
        of_scr[rows, :] = _group_norm(inner + _dot(q, state.astype(BF16)) * qd_f)
        return state * cd_f + kv

    def bwd_local(s, t):
        rows = chunk_rows(s, t)
        q, k, v = q_ref[rows, :], k_ref[rows, :], v_ref[rows, :]
        return rows, q, _dot(pb_scr[rows, :], v), key_value(k, v, kd_b)

    def bwd_finish(local, state):
        rows, q, inner, kv = local
        o = inner + _dot(q, state.astype(BF16)) * qd_b
        y = (gf_ref[rows, :].astype(F32) * of_scr[rows, :]
             + gb_ref[rows, :].astype(F32) * _group_norm(o))
        y_ref[rows, :] = y.astype(BF16)
        return state * cd_b + kv

    def sweep(local_fn, finish_fn, reverse):
        group = min(unroll, n_chunks)

        def run_group(gi):
            items = [(s, gi * group + c) for s in range(n_seq) for c in range(group)]
            states = {}

            def local_at(item):
                s, t = item
                return local_fn(s, n_chunks - 1 - t if reverse else t)

            nxt = local_at(items[0])
            for idx, (s, t) in enumerate(items):
                cur = nxt
                if idx + 1 < len(items):
                    nxt = local_at(items[idx + 1])
                state = states[s] if s in states else st_scr[s]
                states[s] = finish_fn(cur, state)
            for s in range(n_seq):
                st_scr[s] = states[s]

        n_groups = n_chunks // group
        if n_groups == 1:
            run_group(0)
        else:
            lax.fori_loop(0, n_groups, lambda gi, carry: (run_group(gi), carry)[1], 0)

    for s in range(n_seq):
        st_scr[s] = s0f_ref[s, 0] if has_state else jnp.zeros((dk, dk), F32)
    sweep(fwd_local, fwd_finish, reverse=False)
    for s in range(n_seq):
        if emit_state:
            sf_out_ref[s, 0, 0] = st_scr[s]
        st_scr[s] = s0b_ref[s, 0] if has_state else jnp.zeros((dk, dk), F32)
    sweep(bwd_local, bwd_finish, reverse=True)
    if emit_state:
        for s in range(n_seq):
            sb_out_ref[s, 0, 0] = st_scr[s]


def _retention(proj, decay_logit, seq_len, n_seq, state_f=None, state_b=None, emit_state=False):
    t = proj.shape[0]
    d = D_MODEL
    dk = RET_DK
    heads = RET_HEADS
    rows = n_seq * seq_len
    has_state = state_f is not None

    def col(offset):
        return pl.BlockSpec((rows, dk), lambda b, h: (b, offset * heads + h))

    in_specs = [pl.BlockSpec(memory_space=pltpu.SMEM)] + [col(o) for o in range(5)]
    args = [decay_logit, proj, proj, proj, proj, proj]
    if has_state:
        st = pl.BlockSpec((n_seq, 1, dk, dk), lambda b, h: (b, h, 0, 0))
        in_specs += [st, st]
        args += [state_f, state_b]
    out_specs = [pl.BlockSpec((rows, dk), lambda b, h: (b, h))]
    out_shape = [jax.ShapeDtypeStruct((t, d), BF16)]
    if emit_state:
        n_batch = t // seq_len
        st_out = pl.BlockSpec((n_seq, 1, 1, dk, dk), lambda b, h: (b, 0, h, 0, 0))
        out_specs += [st_out, st_out]
        out_shape += [jax.ShapeDtypeStruct((n_batch, 1, heads, dk, dk), F32)] * 2
    return pl.pallas_call(
        functools.partial(_ret_kernel, n_seq=n_seq, seq_len=seq_len,
                          has_state=has_state, emit_state=emit_state, unroll=32),
        grid=(t // rows, heads),
        in_specs=in_specs,
        out_specs=out_specs,
        out_shape=out_shape,
        scratch_shapes=[pltpu.VMEM((rows, dk), F32), pltpu.VMEM((rows, RET_CHUNK), BF16),
                        pltpu.VMEM((n_seq, dk, dk), F32)],
        compiler_params=_cparams(2),
        name="retention_state" if emit_state else "retention",
    )(*args)


def kernel(x_prompt, x_sample, cache_na_k, cache_na_v, state_ret_fwd, state_ret_bwd, c, c_ctx,
           w_ada, b_ada, g_norm, na_w_qkv, na_w_o, na_rpb, ret_w_in, ret_w_o, ret_decay_logit,
           ffn_w_in, ffn_w_out):
    bp, seq, d = x_prompt.shape
    bs, n_tok, _ = x_sample.shape
    past = cache_na_k.shape[2]
    xp = x_prompt.reshape(bp * seq, d)
    xs = x_sample.reshape(bs * n_tok, d)

    cond = jnp.concatenate([c_ctx[None, :], c, jnp.zeros((8 - 1 - bs, d), F32)], axis=0)
    mod = _ada_modulation(cond, w_ada, b_ada).reshape(w_ada.shape[0], 8, 6, d)
    mod_p = [mod[l, 0:1] for l in range(2)]
    mod_s = [mod[l, 1:1 + bs] for l in range(2)]

    w_qkv = na_w_qkv[0].astype(BF16)
    w_na_o = na_w_o[0].astype(BF16)
    w_ret_in = ret_w_in[0].astype(BF16)
    w_ret_o = ret_w_o[0].astype(BF16)
    w_ffn_in = ffn_w_in.astype(BF16)
    w_ffn_out = ffn_w_out

    qkv_p, k_p, v_p = _qkv_proj(xp, mod_p[0], g_norm[0], w_qkv, emit_kv=True)
    (qkv_s,) = _qkv_proj(xs, mod_s[0], g_norm[0], w_qkv, emit_kv=False)
    o_p = _ctx_attention(qkv_p, seq)
    bias = _na_bias_table(na_rpb[0], n_tok // GRID_W)
    ck = cache_na_k[:, 0].reshape(bs, past, d)
    cv = cache_na_v[:, 0].reshape(bs, past, d)
    o_s = _na_attention(qkv_s, ck, cv, bias, bs, n_tok)

    xp = _mix_out(o_p, w_na_o, xp, mod_p[0], g_norm[0])
    xs = _mix_out(o_s, w_na_o, xs, mod_s[0], g_norm[0])
    xp = _ffn(xp, mod_p[0], g_norm[0], w_ffn_in, w_ffn_out, 0)
    xs = _ffn(xs, mod_s[0], g_norm[0], w_ffn_in, w_ffn_out, 0)

    proj_p = _ret_in_proj(xp, mod_p[1], g_norm[1], w_ret_in)
    proj_s = _ret_in_proj(xs, mod_s[1], g_norm[1], w_ret_in, _rope_tables(n_tok, RET_DK), n_tok)
    y_p, sf, sb = _retention(proj_p, ret_decay_logit[0], seq, 16, emit_state=True)
    (y_s,) = _retention(proj_s, ret_decay_logit[0], n_tok, 1,
                        state_f=state_ret_fwd[:, 0], state_b=state_ret_bwd[:, 0])

    xp = _mix_out(y_p, w_ret_o, xp, mod_p[1], g_norm[1])
    xs = _mix_out(y_s, w_ret_o, xs, mod_s[1], g_norm[1])
    xp = _ffn(xp, mod_p[1], g_norm[1], w_ffn_in, w_ffn_out, 1)
    xs = _ffn(xs, mod_s[1], g_norm[1], w_ffn_in, w_ffn_out, 1)

    kv_shape = (bp, 1, seq, NA_HEADS, NA_HEAD_DIM)
    return (xp.reshape(bp, seq, d), xs.reshape(bs, n_tok, d),
            k_p.reshape(kv_shape), v_p.reshape(kv_shape),
            sf.astype(x_prompt.dtype), sb.astype(x_prompt.dtype))
```
